```python
import math
import jax, jax.numpy as jnp
from jax import lax
import numpy as np

D_MODEL = 1024
BATCH = 8
SEQ = 4096
DEPTH = 2

GRID_W = 64
CTX_LEN = 256
N_MOD = 6
EPS = 1e-6
ATTN_WIDTH = D_MODEL // 2
LRU_WIDTH = D_MODEL // 4
SGU_WIDTH = D_MODEL - ATTN_WIDTH - LRU_WIDTH
ATTN_HEADS = 4
ATTN_V_DIM = ATTN_WIDTH // ATTN_HEADS
ATTN_QK_DIM = ATTN_V_DIM // 2
ROPE_AXIS_DIM = ATTN_QK_DIM // 2
ROPE_THETA = 10000.0
Q_BLOCK = 128
LRU_HEADS = 4
LRU_BLOCK = LRU_WIDTH // LRU_HEADS
CONV_WIDTH = 4
CONV_PAD_L = 1
LRU_C = 8.0
SGU_GROUPS = 4
SGU_GROUP_DIM = SGU_WIDTH // SGU_GROUPS
CHUNK = 128
FFN_HIDDEN = -(-8 * D_MODEL // (3 * 256)) * 256
Q0 = 0
K0 = ATTN_WIDTH
V0 = 2 * ATTN_WIDTH
LX0 = 3 * ATTN_WIDTH
LG0 = LX0 + LRU_WIDTH
SU0 = LG0 + LRU_WIDTH
SV0 = SU0 + SGU_WIDTH
PROJ_WIDTH = SV0 + SGU_WIDTH

kernel_name = 'hybrid_diffattn_rglru_sgu_prefix_block'


def rmsnorm(x, g):
    xf = x.astype(jnp.float32)
    y = xf * lax.rsqrt(jnp.mean(xf * xf, axis=-1, keepdims=True) + EPS)
    return (y * g).astype(x.dtype)


def modulate(h, shift, scale):
    return h * (1 + scale) + shift


def axial_rope_tables(t_len):
    rows = t_len // GRID_W
    row = jnp.repeat(jnp.arange(rows), GRID_W).astype(jnp.float32)
    col = jnp.tile(jnp.arange(GRID_W), rows).astype(jnp.float32)
    n = ROPE_AXIS_DIM // 2
    inv = ROPE_THETA ** (-jnp.arange(n, dtype=jnp.float32) / n)
    ang_r = row[:, None] * inv[None, :]
    ang_c = col[:, None] * inv[None, :]
    return (jnp.cos(ang_r), jnp.sin(ang_r), jnp.cos(ang_c), jnp.sin(ang_c))


def _rotate(x, cos, sin):
    n = x.shape[-1] // 2
    x1, x2 = x[..., :n], x[..., n:]
    cos = cos[None, :, None, :]
    sin = sin[None, :, None, :]
    return jnp.concatenate([x1 * cos - x2 * sin, x1 * sin + x2 * cos], axis=-1)


def apply_rope(x, tabs):
    cr, sr, cc, sc = tabs
    xr = _rotate(x[..., :ROPE_AXIS_DIM], cr, sr)
    xc = _rotate(x[..., ROPE_AXIS_DIM:], cc, sc)
    return jnp.concatenate([xr, xc], axis=-1).astype(x.dtype)


def diff_attention(q1, q2, k1, k2, v, lam):
    scale = ATTN_QK_DIM ** -0.5
    s1 = jnp.einsum('bqhd,bkhd->bhqk', q1, k1).astype(jnp.float32) * scale
    s2 = jnp.einsum('bqhd,bkhd->bhqk', q2, k2).astype(jnp.float32) * scale
    w = jax.nn.softmax(s1, axis=-1) - lam * jax.nn.softmax(s2, axis=-1)
    return jnp.einsum('bhqk,bkhe->bqhe', w.astype(v.dtype), v)


def attention_mixer(pl, pc, lam_q, lam_k, g_attn, lam_init, tabs, need_ctx):
    B, T = pl.shape[0], pl.shape[1]
    Tc = pc.shape[1]
    H2 = 2 * ATTN_HEADS
    ql = apply_rope(pl[..., Q0:K0].reshape(B, T, H2, ATTN_QK_DIM), tabs)
    kl = apply_rope(pl[..., K0:V0].reshape(B, T, H2, ATTN_QK_DIM), tabs)
    vl = pl[..., V0:LX0].reshape(B, T, ATTN_HEADS, ATTN_V_DIM)
    qc = pc[..., Q0:K0].reshape(B, Tc, H2, ATTN_QK_DIM)
    kc = pc[..., K0:V0].reshape(B, Tc, H2, ATTN_QK_DIM)
    vc = pc[..., V0:LX0].reshape(B, Tc, ATTN_HEADS, ATTN_V_DIM)
    lq = lam_q.astype(jnp.float32)
    lk = lam_k.astype(jnp.float32)
    lam = jnp.exp(jnp.sum(lq[0] * lk[0])) - jnp.exp(jnp.sum(lq[1] * lk[1])) + lam_init
    k1 = jnp.concatenate([kc[:, :, 0::2], kl[:, :, 0::2]], axis=1)
    k2 = jnp.concatenate([kc[:, :, 1::2], kl[:, :, 1::2]], axis=1)
    v = jnp.concatenate([vc, vl], axis=1)
    nb = T // Q_BLOCK
    qb = ql.reshape(B, nb, Q_BLOCK, H2, ATTN_QK_DIM).swapaxes(0, 1)
    ol = lax.map(lambda q: diff_attention(q[:, :, 0::2], q[:, :, 1::2], k1, k2, v, lam), qb)
    ol = ol.swapaxes(0, 1).reshape(B, T, ATTN_HEADS, ATTN_V_DIM)

    def post(o):
        o = rmsnorm(o, g_attn) * (1.0 - lam_init)
        return o.reshape(o.shape[0], o.shape[1], ATTN_WIDTH)

    yl = post(ol)
    yc = None
    if need_ctx:
        oc = diff_attention(qc[:, :, 0::2], qc[:, :, 1::2], kc[:, :, 0::2], kc[:, :, 1::2], vc, lam)
        yc = post(oc)
    return yl, yc


def short_conv(x, w, b):
    T = x.shape[1]
    xp = jnp.pad(x, ((0, 0), (CONV_PAD_L, CONV_WIDTH - 1 - CONV_PAD_L), (0, 0)))
    return sum(xp[:, k:k + T] * w[k] for k in range(CONV_WIDTH)) + b


def block_diag(x, w):
    B, T = x.shape[0], x.shape[1]
    xh = x.reshape(B, T, LRU_HEADS, LRU_BLOCK)
    return jnp.einsum('bthi,hij->bthj', xh, w).reshape(B, T, LRU_WIDTH)


def _combine(e1, e2):
    a1, b1 = e1
    a2, b2 = e2
    return a1 * a2, a2 * b1 + b2


def linear_scan(a, bt, h0, reverse):
    idx = -1 if reverse else 0
    bt = bt.at[:, idx].add(a[:, idx] * h0)
    _, h = lax.associative_scan(_combine, (a, bt), reverse=reverse, axis=1)
    return h


def rglru_direction(xc, w_a, b_a, w_x, b_x, lam, h0, reverse):
    xf = xc.astype(jnp.float32)
    r = jax.nn.sigmoid(block_diag(xf, w_a) + b_a)
    i = jax.nn.sigmoid(block_diag(xf, w_x) + b_x)
    log_a = -LRU_C * r * jax.nn.softplus(-lam)
    a = jnp.exp(log_a)
    bt = jnp.sqrt(jnp.maximum(-jnp.expm1(2.0 * log_a), 0.0)) * (i * xf)
    h = linear_scan(a, bt, h0, reverse)
    final = h[:, 0] if reverse else h[:, -1]
    return h, final


def rglru_mixer(pl, pc, conv_w, conv_b, w_a, b_a, w_x, b_x, lam, need_ctx):
    cl = short_conv(pl[..., LX0:LG0], conv_w, conv_b)
    cc = short_conv(pc[..., LX0:LG0], conv_w, conv_b)
    B = pl.shape[0]
    h0 = jnp.zeros((B, LRU_WIDTH), jnp.float32)
    hl_sum = 0
    hc_sum = 0
    for d, rev in enumerate((False, True)):
        h_c, s_c = rglru_direction(cc, w_a[d], b_a[d], w_x[d], b_x[d], lam[d], h0, rev)
        h_l, _ = rglru_direction(cl, w_a[d], b_a[d], w_x[d], b_x[d], lam[d], s_c, rev)
        hl_sum = hl_sum + h_l
        hc_sum = hc_sum + h_c
    yl = hl_sum.astype(pl.dtype) * jax.nn.gelu(pl[..., LG0:SU0])
    yc = hc_sum.astype(pc.dtype) * jax.nn.gelu(pc[..., LG0:SU0]) if need_ctx else None
    return yl, yc


def sgu_mixer(p, g, w_s, b_s):
    B, T = p.shape[0], p.shape[1]
    u = jax.nn.gelu(p[..., SU0:SV0])
    v = jax.nn.gelu(p[..., SV0:PROJ_WIDTH])
    vn = rmsnorm(v.reshape(B, T, SGU_GROUPS, SGU_GROUP_DIM), g.reshape(SGU_GROUPS, SGU_GROUP_DIM))
    vc = vn.reshape(B, T // CHUNK, CHUNK, SGU_GROUPS, SGU_GROUP_DIM)
    m = jnp.einsum('gpq,bnqgc->bnpgc', w_s, vc) + b_s.T[:, :, None]
    return u * m.reshape(B, T, SGU_WIDTH)


def swiglu(h, wg, wu, wd):
    return (jax.nn.silu(h @ wg) * (h @ wu)) @ wd


def setup_inputs(seed: int = 0) -> dict:
    key = jax.random.key(seed)
    ks = jax.random.split(key, 28)
    f32 = jnp.float32
    L = DEPTH

    def nrm(k, shape, s):
        return jax.random.normal(k, shape, f32) * s

    u = jax.random.uniform(ks[16], (L, 2, LRU_WIDTH), f32, 0.9, 0.999)
    sig = u ** (1.0 / LRU_C)
    lru_lambda = jnp.log(sig) - jnp.log1p(-sig)
    return {
        'x': nrm(ks[0], (BATCH, SEQ, D_MODEL), 1.0),
        'c': nrm(ks[1], (BATCH, D_MODEL), 1.0),
        'ctx': nrm(ks[2], (BATCH, CTX_LEN, D_MODEL), 1.0),
        'c_ctx': nrm(ks[3], (D_MODEL,), 1.0),
        'w_ada': nrm(ks[4], (L, D_MODEL, N_MOD * D_MODEL), 0.5 * D_MODEL ** -0.5),
        'b_ada': nrm(ks[5], (L, N_MOD * D_MODEL), 0.02),
        'g_norm1': 1.0 + nrm(ks[6], (L, D_MODEL), 0.02),
        'g_norm2': 1.0 + nrm(ks[7], (L, D_MODEL), 0.02),
        'w_in': nrm(ks[8], (L, D_MODEL, PROJ_WIDTH), D_MODEL ** -0.5),
        'lam_q': nrm(ks[9], (L, 2, ATTN_QK_DIM), 0.1),
        'lam_k': nrm(ks[10], (L, 2, ATTN_QK_DIM), 0.1),
        'g_attn': 1.0 + nrm(ks[11], (L, ATTN_V_DIM), 0.02),
        'conv_w': nrm(ks[12], (L, CONV_WIDTH, LRU_WIDTH), CONV_WIDTH ** -0.5),
        'conv_b': nrm(ks[13], (L, LRU_WIDTH), 0.02),
        'w_rg_a': nrm(ks[14], (L, 2, LRU_HEADS, LRU_BLOCK, LRU_BLOCK), LRU_BLOCK ** -0.5),
        'b_rg_a': nrm(ks[15], (L, 2, LRU_WIDTH), 0.02),
        'w_rg_x': nrm(ks[17], (L, 2, LRU_HEADS, LRU_BLOCK, LRU_BLOCK), LRU_BLOCK ** -0.5),
        'b_rg_x': nrm(ks[18], (L, 2, LRU_WIDTH), 0.02),
        'lru_lambda': lru_lambda,
        'g_sgu': 1.0 + nrm(ks[19], (L, SGU_WIDTH), 0.02),
        'w_spatial': nrm(ks[20], (L, SGU_GROUPS, CHUNK, CHUNK), CHUNK ** -0.5),
        'b_spatial': 1.0 + nrm(ks[21], (L, SGU_GROUPS, CHUNK), 0.02),
        'w_out': nrm(ks[22], (L, D_MODEL, D_MODEL), D_MODEL ** -0.5),
        'w_ffn_gate': nrm(ks[23], (L, D_MODEL, FFN_HIDDEN), D_MODEL ** -0.5),
        'w_ffn_up': nrm(ks[24], (L, D_MODEL, FFN_HIDDEN), D_MODEL ** -0.5),
        'w_ffn_down': nrm(ks[25], (L, FFN_HIDDEN, D_MODEL), FFN_HIDDEN ** -0.5),
        'g_final': 1.0 + nrm(ks[26], (D_MODEL,), 0.02),
    }


def reference(x, c, ctx, c_ctx, w_ada, b_ada, g_norm1, g_norm2, w_in, lam_q, lam_k, g_attn,
              conv_w, conv_b, w_rg_a, b_rg_a, w_rg_x, b_rg_x, lru_lambda, g_sgu, w_spatial,
              b_spatial, w_out, w_ffn_gate, w_ffn_up, w_ffn_down, g_final):
    S = x.shape[1]
    tabs = axial_rope_tables(S)
    xl, xc = x, ctx
    for l in range(DEPTH):
        last = l == DEPTH - 1
        lam_init = 0.8 - 0.6 * math.exp(-0.3 * l)
        mod_l = jnp.split((jax.nn.silu(c) @ w_ada[l] + b_ada[l])[:, None, :], N_MOD, axis=-1)
        mod_c = jnp.split(jax.nn.silu(c_ctx) @ w_ada[l] + b_ada[l], N_MOD, axis=-1)
        hl = modulate(rmsnorm(xl, g_norm1[l]), mod_l[0], mod_l[1])
        hc = modulate(rmsnorm(xc, g_norm1[l]), mod_c[0], mod_c[1])
        pl = hl @ w_in[l]
        pc = hc @ w_in[l]
        al, ac = attention_mixer(pl, pc, lam_q[l], lam_k[l], g_attn[l], lam_init, tabs, not last)
        rl, rc = rglru_mixer(pl, pc, conv_w[l], conv_b[l], w_rg_a[l], b_rg_a[l], w_rg_x[l],
                             b_rg_x[l], lru_lambda[l], not last)
        sl = sgu_mixer(pl, g_sgu[l], w_spatial[l], b_spatial[l])
        xl = xl + mod_l[2] * (jnp.concatenate([al, rl, sl], axis=-1) @ w_out[l])
        hl2 = modulate(rmsnorm(xl, g_norm2[l]), mod_l[3], mod_l[4])
        xl = xl + mod_l[5] * swiglu(hl2, w_ffn_gate[l], w_ffn_up[l], w_ffn_down[l])
        if not last:
            sc = sgu_mixer(pc, g_sgu[l], w_spatial[l], b_spatial[l])
            xc = xc + mod_c[2] * (jnp.concatenate([ac, rc, sc], axis=-1) @ w_out[l])
            hc2 = modulate(rmsnorm(xc, g_norm2[l]), mod_c[3], mod_c[4])
            xc = xc + mod_c[5] * swiglu(hc2, w_ffn_gate[l], w_ffn_up[l], w_ffn_down[l])
    return rmsnorm(xl, g_final)
```

```python
import functools
import math

import jax
import jax.numpy as jnp
from jax import lax
from jax.experimental import pallas as pl
from jax.experimental.pallas import tpu as pltpu

F32 = jnp.float32
BF16 = jnp.bfloat16

D_MODEL = 1024
DEPTH = 2
GRID_W = 64
CTX_LEN = 256
N_MOD = 6
EPS = 1e-6
ATTN_WIDTH = 512
LRU_WIDTH = 256
SGU_WIDTH = 256
ATTN_HEADS = 4
HEAD_DIM = 128
QK_DIM = 64
ROPE_AXIS_DIM = 32
ROPE_THETA = 10000.0
LRU_HEADS = 4
LRU_BLOCK = 64
CONV_WIDTH = 4
LRU_C = 8.0
SGU_GROUPS = 4
SGU_GROUP_DIM = 64
CHUNK = 128
FFN_HIDDEN = 2816
PROJ_WIDTH = 2560

TILE = 256
KEY_CHUNK = 2 * TILE
SUBLANES = 8
VMEM_LIMIT = 56 * 1024 * 1024


def _cparams(sem):
    return pltpu.CompilerParams(dimension_semantics=sem, vmem_limit_bytes=VMEM_LIMIT)


def _const_spec(shape):
    nd = len(shape)
    return pl.BlockSpec(shape, lambda *_: (0,) * nd, pipeline_mode=pl.Buffered(1))


def _mod_spec(t_off):
    return pl.BlockSpec((None, N_MOD, 1, D_MODEL),
                        lambda b, i: (jnp.where(i + t_off == 0, 8, b), 0, 0, 0))


def _ada_kernel(c_ref, w_ref, b_ref, o_ref):
    c = c_ref[...]
    h = (c * jax.nn.sigmoid(c)).astype(BF16)
    o_ref[...] = jnp.dot(h, w_ref[...].astype(BF16), preferred_element_type=F32) + b_ref[...]


def _ada(cs, w_ada, b_ada):
    tn = 1536
    n_out = N_MOD * D_MODEL
    return pl.pallas_call(
        _ada_kernel,
        grid=(DEPTH, n_out // tn),
        in_specs=[
            pl.BlockSpec((16, D_MODEL), lambda l, n: (0, 0)),
            pl.BlockSpec((None, D_MODEL, tn), lambda l, n: (l, 0, n)),
            pl.BlockSpec((None, 1, tn), lambda l, n: (l, 0, n)),
        ],
        out_specs=pl.BlockSpec((None, 16, tn), lambda l, n: (l, 0, n)),
        out_shape=jax.ShapeDtypeStruct((DEPTH, 16, n_out), F32),
        compiler_params=_cparams(("arbitrary", "arbitrary")),
        name="ada_mod",
    )(cs, w_ada, b_ada.reshape(DEPTH, 1, n_out))


def _rope(x, cos, sin, first_half):
    partner = jnp.where(first_half, pltpu.roll(x, 128 - 16, 1), pltpu.roll(x, 16, 1))
    return x * cos + partner * sin


def _inproj_kernel(x_ref, mod_ref, g1_ref, w_ref, cos_ref, sin_ref, gs_ref, ones_ref,
                   wsp_ref, bsp_ref, qt_ref, k_ref, vt_ref, lru_ref, sgu_ref):
    x = x_ref[...]
    rs = lax.rsqrt(jnp.mean(x * x, axis=-1, keepdims=True) + EPS)
    h = (x * rs * g1_ref[...]) * (1.0 + mod_ref[1]) + mod_ref[0]
    p = jnp.dot(h.astype(BF16), w_ref[...], preferred_element_type=F32)

    cos = cos_ref[...]
    sin = sin_ref[...]
    lane = lax.broadcasted_iota(jnp.int32, (TILE, HEAD_DIM), 1)
    first_half = (lane & 16) == 0
    scale = QK_DIM ** -0.5
    q_parts, k_parts = [], []
    for hd in range(ATTN_HEADS):
        q = p[:, hd * HEAD_DIM:(hd + 1) * HEAD_DIM]
        k = p[:, ATTN_WIDTH + hd * HEAD_DIM:ATTN_WIDTH + (hd + 1) * HEAD_DIM]
        q_parts.append(_rope(q, cos, sin, first_half) * scale)
        k_parts.append(_rope(k, cos, sin, first_half))
    q = jnp.concatenate(q_parts, axis=1)
    qt_ref[...] = q.T.astype(BF16)
    k_ref[...] = jnp.concatenate(k_parts, axis=1).astype(BF16)
    vt_ref[...] = p[:, 2 * ATTN_WIDTH:3 * ATTN_WIDTH].T.astype(BF16)

    lx0 = 3 * ATTN_WIDTH
    lru_ref[:, 0:LRU_WIDTH] = p[:, lx0:lx0 + LRU_WIDTH]
    lru_ref[:, LRU_WIDTH:] = jax.nn.gelu(p[:, lx0 + LRU_WIDTH:lx0 + 2 * LRU_WIDTH])

    su0 = lx0 + 2 * LRU_WIDTH
    u = jax.nn.gelu(p[:, su0:su0 + SGU_WIDTH])
    v = jax.nn.gelu(p[:, su0 + SGU_WIDTH:su0 + 2 * SGU_WIDTH])
    ss = v * v
    hi = ss.astype(BF16)
    lo = (ss - hi.astype(F32)).astype(BF16)
    gsum = (jnp.dot(hi, ones_ref[...], preferred_element_type=F32)
            + jnp.dot(lo, ones_ref[...], preferred_element_type=F32))
    vn = v * lax.rsqrt(gsum * (1.0 / SGU_GROUP_DIM) + EPS) * gs_ref[...]
    grp = lax.broadcasted_iota(jnp.int32, (CHUNK, SGU_WIDTH), 1) // SGU_GROUP_DIM
    for c in range(TILE // CHUNK):
        vc = vn[c * CHUNK:(c + 1) * CHUNK]
        rhs = jnp.concatenate(
            [jnp.where(grp == g, vc, 0.0) for g in range(SGU_GROUPS)], axis=0).astype(BF16)
        m = jnp.dot(wsp_ref[...], rhs, preferred_element_type=F32) + bsp_ref[...]
        sgu_ref[c * CHUNK:(c + 1) * CHUNK, :] = (u[c * CHUNK:(c + 1) * CHUNK] * m).astype(BF16)


def _inproj(xs, mod, g1, w_in, cos_t, sin_t, g_sgu, ones_bd, w_sp, b_sp):
    B, T, _ = xs.shape
    nt = T // TILE
    return pl.pallas_call(
        _inproj_kernel,
        grid=(B, nt),
        in_specs=[
            pl.BlockSpec((None, TILE, D_MODEL), lambda b, i: (b, i, 0)),
            _mod_spec(0),
            _const_spec((1, D_MODEL)),
            _const_spec((D_MODEL, PROJ_WIDTH)),
            pl.BlockSpec((TILE, HEAD_DIM), lambda b, i: (i, 0)),
            pl.BlockSpec((TILE, HEAD_DIM), lambda b, i: (i, 0)),
            _const_spec((1, SGU_WIDTH)),
            _const_spec((SGU_WIDTH, SGU_WIDTH)),
            _const_spec((CHUNK, SGU_GROUPS * CHUNK)),
            _const_spec((CHUNK, SGU_WIDTH)),
        ],
        out_specs=[
            pl.BlockSpec((None, None, ATTN_WIDTH, TILE), lambda b, i: (b, i, 0, 0)),
            pl.BlockSpec((None, TILE, ATTN_WIDTH), lambda b, i: (b, i, 0)),
            pl.BlockSpec((None, None, ATTN_WIDTH, TILE), lambda b, i: (b, i, 0, 0)),
            pl.BlockSpec((None, TILE, 2 * LRU_WIDTH), lambda b, i: (b, i, 0)),
            pl.BlockSpec((None, TILE, SGU_WIDTH), lambda b, i: (b, i, 0)),
        ],
        out_shape=[
            jax.ShapeDtypeStruct((B, nt, ATTN_WIDTH, TILE), BF16),
            jax.ShapeDtypeStruct((B, T, ATTN_WIDTH), BF16),
            jax.ShapeDtypeStruct((B, nt, ATTN_WIDTH, TILE), BF16),
            jax.ShapeDtypeStruct((B, T, 2 * LRU_WIDTH), F32),
            jax.ShapeDtypeStruct((B, T, SGU_WIDTH), BF16),
        ],
        compiler_params=_cparams(("parallel", "parallel")),
        name="inproj",
    )(xs, mod, g1, w_in, cos_t, sin_t, g_sgu, ones_bd, w_sp, b_sp)


def _attn_kernel(q_ref, k_ref, v_ref, lq_ref, lk_ref, g_ref, o_ref, acc_ref, m_ref, l_ref,
                 *, lam_init, q_off):
    qi = pl.program_id(2) + q_off
    qf = q_ref[...].astype(F32)
    row = lax.broadcasted_iota(jnp.int32, qf.shape, 0)
    rhs = jnp.concatenate([jnp.where(row < QK_DIM, qf, 0.0),
                           jnp.where(row >= QK_DIM, qf, 0.0)], axis=1).astype(BF16)

    s = jnp.dot(k_ref[0:CTX_LEN, :], rhs, preferred_element_type=F32)
    m0 = jnp.max(s, axis=0, keepdims=True)
    p = jnp.exp(s - m0)
    m_ref[...] = m0
    l_ref[...] = jnp.sum(p, axis=0, keepdims=True)
    acc_ref[...] = jnp.dot(v_ref[0], p.astype(BF16), preferred_element_type=F32)

    def step(c, carry):
        k0 = pl.multiple_of(CTX_LEN + c * KEY_CHUNK, TILE)
        s = jnp.dot(k_ref[pl.ds(k0, KEY_CHUNK), :], rhs, preferred_element_type=F32)
        m_old = m_ref[...]
        m_new = jnp.maximum(m_old, jnp.max(s, axis=0, keepdims=True))
        alpha = jnp.exp(m_old - m_new)
        p = jnp.exp(s - m_new)
        l_ref[...] = alpha * l_ref[...] + jnp.sum(p, axis=0, keepdims=True)
        vch = jnp.concatenate([v_ref[2 * c + 1], v_ref[2 * c + 2]], axis=1)
        acc_ref[...] = alpha * acc_ref[...] + jnp.dot(vch, p.astype(BF16),
                                                      preferred_element_type=F32)
        m_ref[...] = m_new
        return carry

    n_latent = (k_ref.shape[0] - CTX_LEN) // KEY_CHUNK
    lax.fori_loop(0, jnp.where(qi == 0, 0, n_latent), step, 0)

    t = jnp.sum(lq_ref[...] * lk_ref[...], axis=1, keepdims=True)
    e = jnp.exp(t)
    lam = e[0:1] - e[1:2] + lam_init
    acc = acc_ref[...]
    inv = 1.0 / l_ref[...]
    ot = acc[:, :TILE] * inv[:, :TILE] - lam * (acc[:, TILE:] * inv[:, TILE:])
    rs = lax.rsqrt(jnp.mean(ot * ot, axis=0, keepdims=True) + EPS)
    y = (ot * rs).T * g_ref[...]
    o_ref[...] = (y * (1.0 - lam_init)).astype(BF16)


def _attention(qt, k, vt, lam_q, lam_k, g_attn, lam_init, q_off):
    B, nt, _, _ = qt.shape
    T = k.shape[1]
    kern = functools.partial(_attn_kernel, lam_init=lam_init, q_off=q_off)
    return pl.pallas_call(
        kern,
        grid=(B, ATTN_HEADS, nt - q_off),
        in_specs=[
            pl.BlockSpec((None, None, HEAD_DIM, TILE), lambda b, h, i: (b, i + q_off, h, 0)),
            pl.BlockSpec((None, T, HEAD_DIM), lambda b, h, i: (b, 0, h)),
            pl.BlockSpec((None, nt, HEAD_DIM, TILE), lambda b, h, i: (b, 0, h, 0)),
            pl.BlockSpec((2, QK_DIM), lambda b, h, i: (0, 0)),
            pl.BlockSpec((2, QK_DIM), lambda b, h, i: (0, 0)),
            pl.BlockSpec((1, HEAD_DIM), lambda b, h, i: (0, 0)),
        ],
        out_specs=pl.BlockSpec((None, TILE, HEAD_DIM), lambda b, h, i: (b, i + q_off, h)),
        out_shape=jax.ShapeDtypeStruct((B, T, ATTN_WIDTH), BF16),
        scratch_shapes=[
            pltpu.VMEM((HEAD_DIM, 2 * TILE), F32),
            pltpu.VMEM((1, 2 * TILE), F32),
            pltpu.VMEM((1, 2 * TILE), F32),
        ],
        compiler_params=_cparams(("parallel", "parallel", "arbitrary")),
        name="diff_attn",
    )(qt, k, vt, lam_q, lam_k, g_attn)


def _expm1(x, u):
    near = jnp.where(u == 1.0, x, (u - 1.0) * x / jnp.log(u))
    return jnp.where(jnp.abs(x) > 0.5, u - 1.0, near)


def _chunk_scan(a, bt, carry, sub, reverse):
    n = a.shape[0]
    acc_a, acc_b = a, bt
    for k in (1, 2, 4):
        if reverse:
            valid = sub <= SUBLANES - 1 - k
            shift = n - k
        else:
            valid = sub >= k
            shift = k
        a_sh = jnp.where(valid, pltpu.roll(acc_a, shift, 0), 1.0)
        b_sh = jnp.where(valid, pltpu.roll(acc_b, shift, 0), 0.0)
        acc_b = acc_a * b_sh + acc_b
        acc_a = acc_a * a_sh
    groups = n // SUBLANES
    order = range(groups - 1, -1, -1) if reverse else range(groups)
    edge = 0 if reverse else SUBLANES - 1
    carries = [None] * groups
    c = carry
    for g in order:
        carries[g] = jnp.broadcast_to(c, (SUBLANES, a.shape[1]))
        r = g * SUBLANES + edge
        c = acc_a[r:r + 1] * c + acc_b[r:r + 1]
    h = acc_a * jnp.concatenate(carries, axis=0) + acc_b
    return h, c


def _lru_kernel(x_ref, cw_ref, cb_ref, wg_ref, bg_ref, lam_ref, o_ref, hs_ref):
    T = x_ref.shape[0]
    nt = T // TILE
    row = lax.broadcasted_iota(jnp.int32, (TILE, LRU_WIDTH), 0)
    sub = row & (SUBLANES - 1)
    cw = cw_ref[...]
    cb = cb_ref[...]

    def conv(j):
        r0 = pl.multiple_of(j * TILE, TILE)
        cur = x_ref[pl.ds(r0, TILE), 0:LRU_WIDTH]
        pr = pl.multiple_of(jnp.maximum(r0 - SUBLANES, 0), SUBLANES)
        nx = pl.multiple_of(jnp.minimum(r0 + TILE, T - SUBLANES), SUBLANES)
        prev8 = x_ref[pl.ds(pr, SUBLANES), 0:LRU_WIDTH]
        next8 = x_ref[pl.ds(nx, SUBLANES), 0:LRU_WIDTH]
        has_prev = j >= 2
        has_next = jnp.logical_and(j >= 1, j <= nt - 2)
        p1 = jnp.where(has_prev, prev8[SUBLANES - 1:SUBLANES], 0.0)
        n0 = jnp.where(has_next, next8[0:1], 0.0)
        n1 = jnp.where(has_next, next8[1:2], 0.0)
        xm1 = jnp.where(row == 0, p1, pltpu.roll(cur, 1, 0))
        xp1 = jnp.where(row == TILE - 1, n0, pltpu.roll(cur, TILE - 1, 0))
        xp2 = jnp.where(row == TILE - 2, n0,
                        jnp.where(row == TILE - 1, n1, pltpu.roll(cur, TILE - 2, 0)))
        return cw[0:1] * xm1 + cw[1:2] * cur + cw[2:3] * xp1 + cw[3:4] * xp2 + cb

    def direction(j, d, carry):
        c = conv(j)
        gates = jnp.dot(c.astype(BF16), wg_ref[d], preferred_element_type=F32) + bg_ref[d]
        r = jax.nn.sigmoid(gates[:, :LRU_WIDTH])
        i = jax.nn.sigmoid(gates[:, LRU_WIDTH:])
        log_a = (-LRU_C * r) * jax.nn.softplus(-lam_ref[d])
        a = jnp.exp(log_a)
        bt = jnp.sqrt(jnp.maximum(-_expm1(2.0 * log_a, a * a), 0.0)) * (i * c)
        return _chunk_scan(a, bt, carry, sub, reverse=(d == 1))

    def fwd(j, carry):
        h, carry = direction(j, 0, carry)
        hs_ref[pl.ds(pl.multiple_of(j * TILE, TILE), TILE), :] = h
        return carry

    def rev(n, carry):
        j = jnp.where(n == 0, 0, nt - n)
        h, carry = direction(j, 1, carry)
        r0 = pl.multiple_of(j * TILE, TILE)
        gate = x_ref[pl.ds(r0, TILE), LRU_WIDTH:]
        o_ref[pl.ds(r0, TILE), :] = ((hs_ref[pl.ds(r0, TILE), :] + h) * gate).astype(BF16)
        return carry

    zero = jnp.zeros((1, LRU_WIDTH), F32)
    lax.fori_loop(0, nt, fwd, zero)
    lax.fori_loop(0, nt, rev, zero)


def _lru(lru_in, conv_w, conv_b, w_gate, b_gate, lam):
    B, T, _ = lru_in.shape
    return pl.pallas_call(
        _lru_kernel,
        grid=(B,),
        in_specs=[
            pl.BlockSpec((None, T, 2 * LRU_WIDTH), lambda b: (b, 0, 0)),
            _const_spec((CONV_WIDTH, LRU_WIDTH)),
            _const_spec((1, LRU_WIDTH)),
            _const_spec((2, LRU_WIDTH, 2 * LRU_WIDTH)),
            _const_spec((2, 1, 2 * LRU_WIDTH)),
            _const_spec((2, 1, LRU_WIDTH)),
        ],
        out_specs=pl.BlockSpec((None, T, LRU_WIDTH), lambda b: (b, 0, 0)),
        out_shape=jax.ShapeDtypeStruct((B, T, LRU_WIDTH), BF16),
        scratch_shapes=[pltpu.VMEM((T, LRU_WIDTH), F32)],
        compiler_params=_cparams(("parallel",)),
        name="rglru",
    )(lru_in, conv_w, conv_b, w_gate, b_gate, lam)


def _ffn_kernel(ya_ref, yr_ref, ys_ref, x_ref, mod_ref, g2_ref, gf_ref, wo_ref, wg_ref, wu_ref,
                wd_ref, o_ref, *, final):
    y = jnp.concatenate([ya_ref[...], yr_ref[...], ys_ref[...]], axis=-1)
    x1 = x_ref[...] + mod_ref[2] * jnp.dot(y, wo_ref[...], preferred_element_type=F32)
    rs = lax.rsqrt(jnp.mean(x1 * x1, axis=-1, keepdims=True) + EPS)
    h = ((x1 * rs * g2_ref[...]) * (1.0 + mod_ref[4]) + mod_ref[3]).astype(BF16)
    g = jnp.dot(h, wg_ref[...], preferred_element_type=F32)
    u = jnp.dot(h, wu_ref[...], preferred_element_type=F32)
    act = ((g * jax.nn.sigmoid(g)) * u).astype(BF16)
    x2 = x1 + mod_ref[5] * jnp.dot(act, wd_ref[...], preferred_element_type=F32)
    if final:
        x2 = x2 * lax.rsqrt(jnp.mean(x2 * x2, axis=-1, keepdims=True) + EPS) * gf_ref[...]
    o_ref[...] = x2


def _ffn(ya, yr, ys, xs, mod, g2, g_final, w_out, w_gate, w_up, w_down, t_off, final):
    B, T, _ = xs.shape
    nt = T // TILE - t_off
    kern = functools.partial(_ffn_kernel, final=final)
    tok = lambda w: pl.BlockSpec((None, TILE, w), lambda b, i: (b, i + t_off, 0))
    return pl.pallas_call(
        kern,
        grid=(B, nt),
        in_specs=[
            tok(ATTN_WIDTH), tok(LRU_WIDTH), tok(SGU_WIDTH), tok(D_MODEL),
            _mod_spec(t_off),
            _const_spec((1, D_MODEL)),
            _const_spec((1, D_MODEL)),
            _const_spec((D_MODEL, D_MODEL)),
            _const_spec((D_MODEL, FFN_HIDDEN)),
            _const_spec((D_MODEL, FFN_HIDDEN)),
            _const_spec((FFN_HIDDEN, D_MODEL)),
        ],
        out_specs=pl.BlockSpec((None, TILE, D_MODEL), lambda b, i: (b, i, 0)),
        out_shape=jax.ShapeDtypeStruct((B, nt * TILE, D_MODEL), F32),
        compiler_params=_cparams(("parallel", "parallel")),
        name="outproj_ffn",
    )(ya, yr, ys, xs, mod, g2, g_final, w_out, w_gate, w_up, w_down)


def _rope_tables(seq):
    rows = seq // GRID_W
    row = jnp.repeat(jnp.arange(rows), GRID_W).astype(F32)
    col = jnp.tile(jnp.arange(GRID_W), rows).astype(F32)
    n = ROPE_AXIS_DIM // 2
    inv = ROPE_THETA ** (-jnp.arange(n, dtype=F32) / n)
    ang_r = row[:, None] * inv[None, :]
    ang_c = col[:, None] * inv[None, :]
    cr, sr, cc, sc = jnp.cos(ang_r), jnp.sin(ang_r), jnp.cos(ang_c), jnp.sin(ang_c)
    cos = jnp.tile(jnp.concatenate([cr, cr, cc, cc], axis=1), (1, 2))
    sin = jnp.tile(jnp.concatenate([-sr, sr, -sc, sc], axis=1), (1, 2))
    cos = jnp.concatenate([jnp.ones((CTX_LEN, HEAD_DIM), F32), cos], axis=0)
    sin = jnp.concatenate([jnp.zeros((CTX_LEN, HEAD_DIM), F32), sin], axis=0)
    return cos, sin


def _block_diag(w):
    eye = jnp.eye(w.shape[0], dtype=w.dtype)
    return jnp.einsum('hij,hk->hikj', w, eye).reshape(w.shape[0] * w.shape[1],
                                                      w.shape[0] * w.shape[2])


def kernel(x, c, ctx, c_ctx, w_ada, b_ada, g_norm1, g_norm2, w_in, lam_q, lam_k, g_attn, conv_w,
           conv_b, w_rg_a, b_rg_a, w_rg_x, b_rg_x, lru_lambda, g_sgu, w_spatial, b_spatial, w_out,
           w_ffn_gate, w_ffn_up, w_ffn_down, g_final):
    B, S, D = x.shape
    assert (D, ctx.shape[1], S % KEY_CHUNK) == (D_MODEL, CTX_LEN, 0) and B <= 8
    cos_t, sin_t = _rope_tables(S)
    cs = jnp.zeros((16, D), F32).at[:B].set(c).at[8].set(c_ctx)
    mods = _ada(cs, w_ada, b_ada)
    ones_bd = _block_diag(jnp.ones((SGU_GROUPS, SGU_GROUP_DIM, SGU_GROUP_DIM), BF16))
    xs = jnp.concatenate([ctx, x], axis=1)

    for l in range(DEPTH):
        last = l == DEPTH - 1
        lam_init = 0.8 - 0.6 * math.exp(-0.3 * l)
        mod = mods[l].reshape(16, N_MOD, 1, D)
        w_sp = jnp.transpose(w_spatial[l], (1, 0, 2)).reshape(CHUNK, SGU_GROUPS * CHUNK)
        b_sp = jnp.repeat(b_spatial[l].T, SGU_GROUP_DIM, axis=1)
        w_gate = jnp.stack([jnp.concatenate([_block_diag(w_rg_a[l, d]), _block_diag(w_rg_x[l, d])],
                                            axis=1) for d in range(2)]).astype(BF16)
        b_gate = jnp.concatenate([b_rg_a[l], b_rg_x[l]], axis=1)[:, None, :]

        qt, k, vt, lru_in, ys = _inproj(
            xs, mod, g_norm1[l][None], w_in[l].astype(BF16), cos_t, sin_t, g_sgu[l][None],
            ones_bd, w_sp.astype(BF16), b_sp)
        t_off = 1 if last else 0
        ya = _attention(qt, k, vt, lam_q[l], lam_k[l], g_attn[l][None], lam_init, t_off)
        yr = _lru(lru_in, conv_w[l], conv_b[l][None], w_gate, b_gate, lru_lambda[l][:, None, :])
        xs = _ffn(ya, yr, ys, xs, mod, g_norm2[l][None], g_final[None], w_out[l].astype(BF16),
                  w_ffn_gate[l].astype(BF16), w_ffn_up[l].astype(BF16),
                  w_ffn_down[l].astype(BF16), t_off, last)
    return xs
```

```python
import functools
import math

import jax
import jax.numpy as jnp
from jax import lax
from jax.experimental import pallas as pl
from jax.experimental.pallas import tpu as pltpu

F32 = jnp.float32
BF16 = jnp.bfloat16

D_MODEL = 1024
DEPTH = 2
GRID_W = 64
CTX_LEN = 256
N_MOD = 6
EPS = 1e-6
ATTN_WIDTH = 512
LRU_WIDTH = 256
SGU_WIDTH = 256
ATTN_HEADS = 4
HEAD_DIM = 128
QK_DIM = 64
ROPE_AXIS_DIM = 32
ROPE_THETA = 10000.0
LRU_HEADS = 4
LRU_BLOCK = 64
CONV_WIDTH = 4
LRU_C = 8.0
SGU_GROUPS = 4
SGU_GROUP_DIM = 64
CHUNK = 128
FFN_HIDDEN = 2816
PROJ_WIDTH = 2560

TILE = 256
KEY_CHUNK = 2 * TILE
SUBLANES = 8
BF16_ROWS = 16
VMEM_LIMIT = 56 * 1024 * 1024


def _cparams(sem):
    return pltpu.CompilerParams(dimension_semantics=sem, vmem_limit_bytes=VMEM_LIMIT)


def _const_spec(shape):
    nd = len(shape)
    return pl.BlockSpec(shape, lambda *_: (0,) * nd, pipeline_mode=pl.Buffered(1))


def _mod_spec(t_off):
    return pl.BlockSpec((None, N_MOD, 1, D_MODEL),
                        lambda b, i: (jnp.where(i + t_off == 0, 8, b), 0, 0, 0))


def _ada_kernel(c_ref, w_ref, b_ref, o_ref):
    c = c_ref[...]
    h = (c * jax.nn.sigmoid(c)).astype(BF16)
    o_ref[...] = jnp.dot(h, w_ref[...].astype(BF16), preferred_element_type=F32) + b_ref[...]


def _ada(cs, w_ada, b_ada):
    tn = 1536
    n_out = N_MOD * D_MODEL
    return pl.pallas_call(
        _ada_kernel,
        grid=(DEPTH, n_out // tn),
        in_specs=[
            pl.BlockSpec((16, D_MODEL), lambda l, n: (0, 0)),
            pl.BlockSpec((None, D_MODEL, tn), lambda l, n: (l, 0, n)),
            pl.BlockSpec((None, 1, tn), lambda l, n: (l, 0, n)),
        ],
        out_specs=pl.BlockSpec((None, 16, tn), lambda l, n: (l, 0, n)),
        out_shape=jax.ShapeDtypeStruct((DEPTH, 16, n_out), F32),
        compiler_params=_cparams(("arbitrary", "arbitrary")),
        name="ada_mod",
    )(cs, w_ada, b_ada.reshape(DEPTH, 1, n_out))


def _rope(x, cos, sin, first_half):
    partner = jnp.where(first_half, pltpu.roll(x, 128 - 16, 1), pltpu.roll(x, 16, 1))
    return x * cos + partner * sin


def _inproj_kernel(x_ref, mod_ref, g1_ref, w_ref, cos_ref, sin_ref, gs_ref, ones_ref,
                   wsp_ref, bsp_ref, qt_ref, k_ref, vt_ref, lru_ref, sgu_ref):
    x = x_ref[...]
    rs = lax.rsqrt(jnp.mean(x * x, axis=-1, keepdims=True) + EPS)
    h = (x * rs * g1_ref[...]) * (1.0 + mod_ref[1]) + mod_ref[0]
    p = jnp.dot(h.astype(BF16), w_ref[...], preferred_element_type=F32)

    cos = cos_ref[...]
    sin = sin_ref[...]
    lane = lax.broadcasted_iota(jnp.int32, (TILE, HEAD_DIM), 1)
    first_half = (lane & 16) == 0
    scale = QK_DIM ** -0.5 * math.log2(math.e)
    q_parts, k_parts = [], []
    for hd in range(ATTN_HEADS):
        q = p[:, hd * HEAD_DIM:(hd + 1) * HEAD_DIM]
        k = p[:, ATTN_WIDTH + hd * HEAD_DIM:ATTN_WIDTH + (hd + 1) * HEAD_DIM]
        q_parts.append(_rope(q, cos, sin, first_half) * scale)
        k_parts.append(_rope(k, cos, sin, first_half))
    q = jnp.concatenate(q_parts, axis=1)
    qt_ref[...] = q.T.astype(BF16)
    k_ref[...] = jnp.concatenate(k_parts, axis=1).astype(BF16)
    vt_ref[...] = p[:, 2 * ATTN_WIDTH:3 * ATTN_WIDTH].T.astype(BF16)

    lx0 = 3 * ATTN_WIDTH
    lru_ref[:, 0:LRU_WIDTH] = p[:, lx0:lx0 + LRU_WIDTH]
    lru_ref[:, LRU_WIDTH:] = jax.nn.gelu(p[:, lx0 + LRU_WIDTH:lx0 + 2 * LRU_WIDTH])

    su0 = lx0 + 2 * LRU_WIDTH
    u = jax.nn.gelu(p[:, su0:su0 + SGU_WIDTH])
    v = jax.nn.gelu(p[:, su0 + SGU_WIDTH:su0 + 2 * SGU_WIDTH])
    ss = v * v
    hi = ss.astype(BF16)
    lo = (ss - hi.astype(F32)).astype(BF16)
    gsum = (jnp.dot(hi, ones_ref[...], preferred_element_type=F32)
            + jnp.dot(lo, ones_ref[...], preferred_element_type=F32))
    vn = v * lax.rsqrt(gsum * (1.0 / SGU_GROUP_DIM) + EPS) * gs_ref[...]
    grp = lax.broadcasted_iota(jnp.int32, (CHUNK, SGU_WIDTH), 1) // SGU_GROUP_DIM
    for c in range(TILE // CHUNK):
        vc = vn[c * CHUNK:(c + 1) * CHUNK]
        rhs = jnp.concatenate(
            [jnp.where(grp == g, vc, 0.0) for g in range(SGU_GROUPS)], axis=0).astype(BF16)
        m = jnp.dot(wsp_ref[...], rhs, preferred_element_type=F32) + bsp_ref[...]
        sgu_ref[c * CHUNK:(c + 1) * CHUNK, :] = (u[c * CHUNK:(c + 1) * CHUNK] * m).astype(BF16)


def _inproj(xs, mod, g1, w_in, cos_t, sin_t, g_sgu, ones_bd, w_sp, b_sp):
    B, T, _ = xs.shape
    nt = T // TILE
    return pl.pallas_call(
        _inproj_kernel,
        grid=(B, nt),
        in_specs=[
            pl.BlockSpec((None, TILE, D_MODEL), lambda b, i: (b, i, 0)),
            _mod_spec(0),
            _const_spec((1, D_MODEL)),
            _const_spec((D_MODEL, PROJ_WIDTH)),
            pl.BlockSpec((TILE, HEAD_DIM), lambda b, i: (i, 0)),
            pl.BlockSpec((TILE, HEAD_DIM), lambda b, i: (i, 0)),
            _const_spec((1, SGU_WIDTH)),
            _const_spec((SGU_WIDTH, SGU_WIDTH)),
            _const_spec((CHUNK, SGU_GROUPS * CHUNK)),
            _const_spec((CHUNK, SGU_WIDTH)),
        ],
        out_specs=[
            pl.BlockSpec((None, None, ATTN_WIDTH, TILE), lambda b, i: (b, i, 0, 0)),
            pl.BlockSpec((None, TILE, ATTN_WIDTH), lambda b, i: (b, i, 0)),
            pl.BlockSpec((None, None, ATTN_WIDTH, TILE), lambda b, i: (b, i, 0, 0)),
            pl.BlockSpec((None, TILE, 2 * LRU_WIDTH), lambda b, i: (b, i, 0)),
            pl.BlockSpec((None, TILE, SGU_WIDTH), lambda b, i: (b, i, 0)),
        ],
        out_shape=[
            jax.ShapeDtypeStruct((B, nt, ATTN_WIDTH, TILE), BF16),
            jax.ShapeDtypeStruct((B, T, ATTN_WIDTH), BF16),
            jax.ShapeDtypeStruct((B, nt, ATTN_WIDTH, TILE), BF16),
            jax.ShapeDtypeStruct((B, T, 2 * LRU_WIDTH), F32),
            jax.ShapeDtypeStruct((B, T, SGU_WIDTH), BF16),
        ],
        compiler_params=_cparams(("parallel", "parallel")),
        name="inproj",
    )(xs, mod, g1, w_in, cos_t, sin_t, g_sgu, ones_bd, w_sp, b_sp)


def _attn_kernel(q_ref, k_ref, v_ref, lq_ref, lk_ref, g_ref, o_ref, acc_ref, m_ref, cmax_ref,
                 alpha_ref, s_ref, p_ref, *, lam_init, q_off):
    qi = pl.program_id(2) + q_off
    qf = q_ref[...].astype(F32)
    row = lax.broadcasted_iota(jnp.int32, qf.shape, 0)
    rhs = jnp.concatenate([jnp.where(row < QK_DIM, qf, 0.0),
                           jnp.where(row >= QK_DIM, qf, 0.0)], axis=1).astype(BF16)

    n_latent = (k_ref.shape[0] - CTX_LEN) // KEY_CHUNK
    chunks = [(0, CTX_LEN)] + [(CTX_LEN + c * KEY_CHUNK, KEY_CHUNK) for c in range(n_latent)]

    def scores(c):
        k0, kc = chunks[c]
        s = jnp.dot(k_ref[k0:k0 + kc, :], rhs, preferred_element_type=F32)
        s_ref[c % 2, 0:kc, :] = s
        cmax_ref[c % 2] = jnp.max(s, axis=0, keepdims=True)

    def softmax(c):
        kc = chunks[c][1]
        m_new = cmax_ref[c % 2]
        if c > 0:
            m_old = m_ref[...]
            m_new = jnp.maximum(m_old, m_new)
            alpha_ref[c % 2] = jnp.exp2(m_old - m_new)
        m_ref[...] = m_new
        p = jnp.exp2(s_ref[c % 2, 0:kc, :] - m_new)
        p_ref[c % 2, 0:kc, :] = p.astype(BF16)

    def values(c):
        kc = chunks[c][1]
        v = v_ref[0] if c == 0 else jnp.concatenate([v_ref[2 * c - 1], v_ref[2 * c]], axis=1)
        v = jnp.concatenate([v, jnp.ones((BF16_ROWS, kc), BF16)], axis=0)
        upd = jnp.dot(v, p_ref[c % 2, 0:kc, :], preferred_element_type=F32)
        acc_ref[...] = upd if c == 0 else alpha_ref[c % 2] * acc_ref[...] + upd

    def pipeline(n):
        scores(0)
        for c in range(n):
            if c + 1 < n:
                scores(c + 1)
            softmax(c)
            if c > 0:
                values(c - 1)
        values(n - 1)

    if q_off > 0:
        pipeline(len(chunks))
    else:
        pl.when(qi == 0)(lambda: pipeline(1))
        pl.when(qi != 0)(lambda: pipeline(len(chunks)))

    t = jnp.sum(lq_ref[...] * lk_ref[...], axis=1, keepdims=True)
    e = jnp.exp(t)
    lam = e[0:1] - e[1:2] + lam_init
    acc = acc_ref[0:HEAD_DIM, :]
    inv = 1.0 / acc_ref[HEAD_DIM:HEAD_DIM + 1, :]
    ot = acc[:, :TILE] * inv[:, :TILE] - lam * (acc[:, TILE:] * inv[:, TILE:])
    rs = lax.rsqrt(jnp.mean(ot * ot, axis=0, keepdims=True) + EPS)
    y = (ot * rs).T * g_ref[...]
    o_ref[...] = (y * (1.0 - lam_init)).astype(BF16)


def _attention(qt, k, vt, lam_q, lam_k, g_attn, lam_init, q_off):
    B, nt, _, _ = qt.shape
    T = k.shape[1]
    kern = functools.partial(_attn_kernel, lam_init=lam_init, q_off=q_off)
    return pl.pallas_call(
        kern,
        grid=(B, ATTN_HEADS, nt - q_off),
        in_specs=[
            pl.BlockSpec((None, None, HEAD_DIM, TILE), lambda b, h, i: (b, i + q_off, h, 0)),
            pl.BlockSpec((None, T, HEAD_DIM), lambda b, h, i: (b, 0, h)),
            pl.BlockSpec((None, nt, HEAD_DIM, TILE), lambda b, h, i: (b, 0, h, 0)),
            pl.BlockSpec((2, QK_DIM), lambda b, h, i: (0, 0)),
            pl.BlockSpec((2, QK_DIM), lambda b, h, i: (0, 0)),
            pl.BlockSpec((1, HEAD_DIM), lambda b, h, i: (0, 0)),
        ],
        out_specs=pl.BlockSpec((None, TILE, HEAD_DIM), lambda b, h, i: (b, i + q_off, h)),
        out_shape=jax.ShapeDtypeStruct((B, T, ATTN_WIDTH), BF16),
        scratch_shapes=[
            pltpu.VMEM((HEAD_DIM + BF16_ROWS, 2 * TILE), F32),
            pltpu.VMEM((1, 2 * TILE), F32),
            pltpu.VMEM((2, 1, 2 * TILE), F32),
            pltpu.VMEM((2, 1, 2 * TILE), F32),
            pltpu.VMEM((2, KEY_CHUNK, 2 * TILE), F32),
            pltpu.VMEM((2, KEY_CHUNK, 2 * TILE), BF16),
        ],
        compiler_params=_cparams(("parallel", "parallel", "arbitrary")),
        name="diff_attn",
    )(qt, k, vt, lam_q, lam_k, g_attn)


def _expm1(x, u):
    near = jnp.where(u == 1.0, x, (u - 1.0) * x / jnp.log(u))
    return jnp.where(jnp.abs(x) > 0.5, u - 1.0, near)


def _chunk_scan(a, bt, carry, sub, reverse):
    n = a.shape[0]
    acc_a, acc_b = a, bt
    for k in (1, 2, 4):
        if reverse:
            valid = sub <= SUBLANES - 1 - k
            shift = n - k
        else:
            valid = sub >= k
            shift = k
        a_sh = jnp.where(valid, pltpu.roll(acc_a, shift, 0), 1.0)
        b_sh = jnp.where(valid, pltpu.roll(acc_b, shift, 0), 0.0)
        acc_b = acc_a * b_sh + acc_b
        acc_a = acc_a * a_sh
    groups = n // SUBLANES
    order = range(groups - 1, -1, -1) if reverse else range(groups)
    edge = 0 if reverse else SUBLANES - 1
    carries = [None] * groups
    c = carry
    for g in order:
        carries[g] = jnp.broadcast_to(c, (SUBLANES, a.shape[1]))
        r = g * SUBLANES + edge
        c = acc_a[r:r + 1] * c + acc_b[r:r + 1]
    h = acc_a * jnp.concatenate(carries, axis=0) + acc_b
    return h, c


def _lru_kernel(x_ref, cw_ref, cb_ref, wg_ref, bg_ref, lam_ref, o_ref, hs_ref):
    T = x_ref.shape[0]
    nt = T // TILE
    row = lax.broadcasted_iota(jnp.int32, (TILE, LRU_WIDTH), 0)
    sub = row & (SUBLANES - 1)
    cw = cw_ref[...]
    cb = cb_ref[...]

    def conv(j):
        r0 = pl.multiple_of(j * TILE, TILE)
        cur = x_ref[pl.ds(r0, TILE), 0:LRU_WIDTH]
        pr = pl.multiple_of(jnp.maximum(r0 - SUBLANES, 0), SUBLANES)
        nx = pl.multiple_of(jnp.minimum(r0 + TILE, T - SUBLANES), SUBLANES)
        prev8 = x_ref[pl.ds(pr, SUBLANES), 0:LRU_WIDTH]
        next8 = x_ref[pl.ds(nx, SUBLANES), 0:LRU_WIDTH]
        has_prev = j >= 2
        has_next = jnp.logical_and(j >= 1, j <= nt - 2)
        p1 = jnp.where(has_prev, prev8[SUBLANES - 1:SUBLANES], 0.0)
        n0 = jnp.where(has_next, next8[0:1], 0.0)
        n1 = jnp.where(has_next, next8[1:2], 0.0)
        xm1 = jnp.where(row == 0, p1, pltpu.roll(cur, 1, 0))
        xp1 = jnp.where(row == TILE - 1, n0, pltpu.roll(cur, TILE - 1, 0))
        xp2 = jnp.where(row == TILE - 2, n0,
                        jnp.where(row == TILE - 1, n1, pltpu.roll(cur, TILE - 2, 0)))
        return cw[0:1] * xm1 + cw[1:2] * cur + cw[2:3] * xp1 + cw[3:4] * xp2 + cb

    def direction(j, d, carry):
        c = conv(j)
        gates = jnp.dot(c.astype(BF16), wg_ref[d], preferred_element_type=F32) + bg_ref[d]
        r = jax.nn.sigmoid(gates[:, :LRU_WIDTH])
        i = jax.nn.sigmoid(gates[:, LRU_WIDTH:])
        log_a = (-LRU_C * r) * jax.nn.softplus(-lam_ref[d])
        a = jnp.exp(log_a)
        bt = jnp.sqrt(jnp.maximum(-_expm1(2.0 * log_a, a * a), 0.0)) * (i * c)
        return _chunk_scan(a, bt, carry, sub, reverse=(d == 1))

    def fwd(j, carry):
        h, carry = direction(j, 0, carry)
        hs_ref[pl.ds(pl.multiple_of(j * TILE, TILE), TILE), :] = h
        return carry

    def rev(n, carry):
        j = jnp.where(n == 0, 0, nt - n)
        h, carry = direction(j, 1, carry)
        r0 = pl.multiple_of(j * TILE, TILE)
        gate = x_ref[pl.ds(r0, TILE), LRU_WIDTH:]
        o_ref[pl.ds(r0, TILE), :] = ((hs_ref[pl.ds(r0, TILE), :] + h) * gate).astype(BF16)
        return carry

    zero = jnp.zeros((1, LRU_WIDTH), F32)
    lax.fori_loop(0, nt, fwd, zero)
    lax.fori_loop(0, nt, rev, zero)


def _lru(lru_in, conv_w, conv_b, w_gate, b_gate, lam):
    B, T, _ = lru_in.shape
    return pl.pallas_call(
        _lru_kernel,
        grid=(B,),
        in_specs=[
            pl.BlockSpec((None, T, 2 * LRU_WIDTH), lambda b: (b, 0, 0)),
            _const_spec((CONV_WIDTH, LRU_WIDTH)),
            _const_spec((1, LRU_WIDTH)),
            _const_spec((2, LRU_WIDTH, 2 * LRU_WIDTH)),
            _const_spec((2, 1, 2 * LRU_WIDTH)),
            _const_spec((2, 1, LRU_WIDTH)),
        ],
        out_specs=pl.BlockSpec((None, T, LRU_WIDTH), lambda b: (b, 0, 0)),
        out_shape=jax.ShapeDtypeStruct((B, T, LRU_WIDTH), BF16),
        scratch_shapes=[pltpu.VMEM((T, LRU_WIDTH), F32)],
        compiler_params=_cparams(("parallel",)),
        name="rglru",
    )(lru_in, conv_w, conv_b, w_gate, b_gate, lam)


def _ffn_kernel(ya_ref, yr_ref, ys_ref, x_ref, mod_ref, g2_ref, gf_ref, wo_ref, wg_ref, wu_ref,
                wd_ref, o_ref, *, final):
    y = jnp.concatenate([ya_ref[...], yr_ref[...], ys_ref[...]], axis=-1)
    x1 = x_ref[...] + mod_ref[2] * jnp.dot(y, wo_ref[...], preferred_element_type=F32)
    rs = lax.rsqrt(jnp.mean(x1 * x1, axis=-1, keepdims=True) + EPS)
    h = ((x1 * rs * g2_ref[...]) * (1.0 + mod_ref[4]) + mod_ref[3]).astype(BF16)
    g = jnp.dot(h, wg_ref[...], preferred_element_type=F32)
    u = jnp.dot(h, wu_ref[...], preferred_element_type=F32)
    act = ((g * jax.nn.sigmoid(g)) * u).astype(BF16)
    x2 = x1 + mod_ref[5] * jnp.dot(act, wd_ref[...], preferred_element_type=F32)
    if final:
        x2 = x2 * lax.rsqrt(jnp.mean(x2 * x2, axis=-1, keepdims=True) + EPS) * gf_ref[...]
    o_ref[...] = x2


def _ffn(ya, yr, ys, xs, mod, g2, g_final, w_out, w_gate, w_up, w_down, t_off, final):
    B, T, _ = xs.shape
    nt = T // TILE - t_off
    kern = functools.partial(_ffn_kernel, final=final)
    tok = lambda w: pl.BlockSpec((None, TILE, w), lambda b, i: (b, i + t_off, 0))
    return pl.pallas_call(
        kern,
        grid=(B, nt),
        in_specs=[
            tok(ATTN_WIDTH), tok(LRU_WIDTH), tok(SGU_WIDTH), tok(D_MODEL),
            _mod_spec(t_off),
            _const_spec((1, D_MODEL)),
            _const_spec((1, D_MODEL)),
            _const_spec((D_MODEL, D_MODEL)),
            _const_spec((D_MODEL, FFN_HIDDEN)),
            _const_spec((D_MODEL, FFN_HIDDEN)),
            _const_spec((FFN_HIDDEN, D_MODEL)),
        ],
        out_specs=pl.BlockSpec((None, TILE, D_MODEL), lambda b, i: (b, i, 0)),
        out_shape=jax.ShapeDtypeStruct((B, nt * TILE, D_MODEL), F32),
        compiler_params=_cparams(("parallel", "parallel")),
        name="outproj_ffn",
    )(ya, yr, ys, xs, mod, g2, g_final, w_out, w_gate, w_up, w_down)


def _rope_tables(seq):
    rows = seq // GRID_W
    row = jnp.repeat(jnp.arange(rows), GRID_W).astype(F32)
    col = jnp.tile(jnp.arange(GRID_W), rows).astype(F32)
    n = ROPE_AXIS_DIM // 2
    inv = ROPE_THETA ** (-jnp.arange(n, dtype=F32) / n)
    ang_r = row[:, None] * inv[None, :]
    ang_c = col[:, None] * inv[None, :]
    cr, sr, cc, sc = jnp.cos(ang_r), jnp.sin(ang_r), jnp.cos(ang_c), jnp.sin(ang_c)
    cos = jnp.tile(jnp.concatenate([cr, cr, cc, cc], axis=1), (1, 2))
    sin = jnp.tile(jnp.concatenate([-sr, sr, -sc, sc], axis=1), (1, 2))
    cos = jnp.concatenate([jnp.ones((CTX_LEN, HEAD_DIM), F32), cos], axis=0)
    sin = jnp.concatenate([jnp.zeros((CTX_LEN, HEAD_DIM), F32), sin], axis=0)
    return cos, sin


def _block_diag(w):
    eye = jnp.eye(w.shape[0], dtype=w.dtype)
    return jnp.einsum('hij,hk->hikj', w, eye).reshape(w.shape[0] * w.shape[1],
                                                      w.shape[0] * w.shape[2])


def kernel(x, c, ctx, c_ctx, w_ada, b_ada, g_norm1, g_norm2, w_in, lam_q, lam_k, g_attn, conv_w,
           conv_b, w_rg_a, b_rg_a, w_rg_x, b_rg_x, lru_lambda, g_sgu, w_spatial, b_spatial, w_out,
           w_ffn_gate, w_ffn_up, w_ffn_down, g_final):
    B, S, D = x.shape
    assert (D, ctx.shape[1], S % KEY_CHUNK) == (D_MODEL, CTX_LEN, 0) and B <= 8
    cos_t, sin_t = _rope_tables(S)
    cs = jnp.zeros((16, D), F32).at[:B].set(c).at[8].set(c_ctx)
    mods = _ada(cs, w_ada, b_ada)
    ones_bd = _block_diag(jnp.ones((SGU_GROUPS, SGU_GROUP_DIM, SGU_GROUP_DIM), BF16))
    xs = jnp.concatenate([ctx, x], axis=1)

    for l in range(DEPTH):
        last = l == DEPTH - 1
        lam_init = 0.8 - 0.6 * math.exp(-0.3 * l)
        mod = mods[l].reshape(16, N_MOD, 1, D)
        w_sp = jnp.transpose(w_spatial[l], (1, 0, 2)).reshape(CHUNK, SGU_GROUPS * CHUNK)
        b_sp = jnp.repeat(b_spatial[l].T, SGU_GROUP_DIM, axis=1)
        w_gate = jnp.stack([jnp.concatenate([_block_diag(w_rg_a[l, d]), _block_diag(w_rg_x[l, d])],
                                            axis=1) for d in range(2)]).astype(BF16)
        b_gate = jnp.concatenate([b_rg_a[l], b_rg_x[l]], axis=1)[:, None, :]

        qt, k, vt, lru_in, ys = _inproj(
            xs, mod, g_norm1[l][None], w_in[l].astype(BF16), cos_t, sin_t, g_sgu[l][None],
            ones_bd, w_sp.astype(BF16), b_sp)
        t_off = 1 if last else 0
        ya = _attention(qt, k, vt, lam_q[l], lam_k[l], g_attn[l][None], lam_init, t_off)
        yr = _lru(lru_in, conv_w[l], conv_b[l][None], w_gate, b_gate, lru_lambda[l][:, None, :])
        xs = _ffn(ya, yr, ys, xs, mod, g_norm2[l][None], g_final[None], w_out[l].astype(BF16),
                  w_ffn_gate[l].astype(BF16), w_ffn_up[l].astype(BF16),
                  w_ffn_down[l].astype(BF16), t_off, last)
    return xs
```

```python
import functools
import math

import jax
import jax.numpy as jnp
from jax import lax
from jax.experimental import pallas as pl
from jax.experimental.pallas import tpu as pltpu

F32 = jnp.float32
BF16 = jnp.bfloat16

D_MODEL = 1024
DEPTH = 2
GRID_W = 64
CTX_LEN = 256
N_MOD = 6
EPS = 1e-6
ATTN_WIDTH = 512
LRU_WIDTH = 256
SGU_WIDTH = 256
ATTN_HEADS = 4
ATTN_HEADS_PER_STEP = 4
HEAD_DIM = 128
QK_DIM = 64
ROPE_AXIS_DIM = 32
ROPE_THETA = 10000.0
LRU_HEADS = 4
LRU_BLOCK = 64
CONV_WIDTH = 4
LRU_C = 8.0
SGU_GROUPS = 4
SGU_GROUP_DIM = 64
CHUNK = 128
FFN_HIDDEN = 2816
PROJ_WIDTH = 2560

TILE = 256
KEY_CHUNK = 2 * TILE
SUBLANES = 8
BF16_ROWS = 16
VMEM_LIMIT = 56 * 1024 * 1024


def _cparams(sem, flags=None):
    return pltpu.CompilerParams(dimension_semantics=sem, vmem_limit_bytes=VMEM_LIMIT, flags=flags)


def _const_spec(shape):
    nd = len(shape)
    return pl.BlockSpec(shape, lambda *_: (0,) * nd, pipeline_mode=pl.Buffered(1))


def _stream_specs(lat_first, t_off):
    ctx_spec = pl.BlockSpec((None, TILE, D_MODEL), lambda b, i: (b, 0, 0))
    lat_spec = pl.BlockSpec((None, TILE, D_MODEL),
                            lambda b, i: (b, jnp.maximum(i + t_off, 1) - lat_first, 0))
    return ctx_spec, lat_spec


def _stream_tile(ctx_ref, lat_ref, t_off):
    return jnp.where(pl.program_id(1) + t_off == 0, ctx_ref[...], lat_ref[...])


def _mod_spec(t_off):
    return pl.BlockSpec((None, N_MOD, 1, D_MODEL),
                        lambda b, i: (jnp.where(i + t_off == 0, 8, b), 0, 0, 0))


def _ada_kernel(c_ref, w_ref, b_ref, o_ref):
    c = c_ref[...]
    h = (c * jax.nn.sigmoid(c)).astype(BF16)
    o_ref[...] = jnp.dot(h, w_ref[...].astype(BF16), preferred_element_type=F32) + b_ref[...]


def _ada(cs, w_ada, b_ada):
    tn = 1536
    n_out = N_MOD * D_MODEL
    return pl.pallas_call(
        _ada_kernel,
        grid=(DEPTH, n_out // tn),
        in_specs=[
            pl.BlockSpec((16, D_MODEL), lambda l, n: (0, 0)),
            pl.BlockSpec((None, D_MODEL, tn), lambda l, n: (l, 0, n)),
            pl.BlockSpec((None, 1, tn), lambda l, n: (l, 0, n)),
        ],
        out_specs=pl.BlockSpec((None, 16, tn), lambda l, n: (l, 0, n)),
        out_shape=jax.ShapeDtypeStruct((DEPTH, 16, n_out), F32),
        compiler_params=_cparams(("arbitrary", "arbitrary")),
        name="ada_mod",
    )(cs, w_ada, b_ada.reshape(DEPTH, 1, n_out))


def _rope(x, cos, sin, first_half):
    partner = jnp.where(first_half, pltpu.roll(x, 128 - 16, 1), pltpu.roll(x, 16, 1))
    return x * cos + partner * sin


def _inproj_kernel(xc_ref, xl_ref, mod_ref, g1_ref, w_ref, cos_ref, sin_ref, gs_ref, ones_ref,
                   wsp_ref, bsp_ref, qt_ref, k_ref, vt_ref, lru_ref, sgu_ref):
    x = _stream_tile(xc_ref, xl_ref, 0)
    rs = lax.rsqrt(jnp.mean(x * x, axis=-1, keepdims=True) + EPS)
    h = (x * rs * g1_ref[...]) * (1.0 + mod_ref[1]) + mod_ref[0]
    p = jnp.dot(h.astype(BF16), w_ref[...], preferred_element_type=F32)

    cos = cos_ref[...]
    sin = sin_ref[...]
    lane = lax.broadcasted_iota(jnp.int32, (TILE, HEAD_DIM), 1)
    first_half = (lane & 16) == 0
    scale = QK_DIM ** -0.5 * math.log2(math.e)
    q_parts, k_parts = [], []
    for hd in range(ATTN_HEADS):
        q = p[:, hd * HEAD_DIM:(hd + 1) * HEAD_DIM]
        k = p[:, ATTN_WIDTH + hd * HEAD_DIM:ATTN_WIDTH + (hd + 1) * HEAD_DIM]
        q_parts.append(_rope(q, cos, sin, first_half) * scale)
        k_parts.append(_rope(k, cos, sin, first_half))
    q = jnp.concatenate(q_parts, axis=1)
    qt_ref[...] = q.T.astype(BF16)
    k_ref[...] = jnp.concatenate(k_parts, axis=1).astype(BF16)
    vt_ref[...] = p[:, 2 * ATTN_WIDTH:3 * ATTN_WIDTH].T.astype(BF16)

    lx0 = 3 * ATTN_WIDTH
    lru_ref[:, 0:LRU_WIDTH] = p[:, lx0:lx0 + LRU_WIDTH]
    lru_ref[:, LRU_WIDTH:] = jax.nn.gelu(p[:, lx0 + LRU_WIDTH:lx0 + 2 * LRU_WIDTH])

    su0 = lx0 + 2 * LRU_WIDTH
    u = jax.nn.gelu(p[:, su0:su0 + SGU_WIDTH])
    v = jax.nn.gelu(p[:, su0 + SGU_WIDTH:su0 + 2 * SGU_WIDTH])
    ss = v * v
    hi = ss.astype(BF16)
    lo = (ss - hi.astype(F32)).astype(BF16)
    gsum = (jnp.dot(hi, ones_ref[...], preferred_element_type=F32)
            + jnp.dot(lo, ones_ref[...], preferred_element_type=F32))
    vn = v * lax.rsqrt(gsum * (1.0 / SGU_GROUP_DIM) + EPS) * gs_ref[...]
    grp = lax.broadcasted_iota(jnp.int32, (CHUNK, SGU_WIDTH), 1) // SGU_GROUP_DIM
    for c in range(TILE // CHUNK):
        vc = vn[c * CHUNK:(c + 1) * CHUNK]
        rhs = jnp.concatenate(
            [jnp.where(grp == g, vc, 0.0) for g in range(SGU_GROUPS)], axis=0).astype(BF16)
        m = jnp.dot(wsp_ref[...], rhs, preferred_element_type=F32) + bsp_ref[...]
        sgu_ref[c * CHUNK:(c + 1) * CHUNK, :] = (u[c * CHUNK:(c + 1) * CHUNK] * m).astype(BF16)


def _inproj(x_ctx, x_lat, lat_first, mod, g1, w_in, cos_t, sin_t, g_sgu, ones_bd, w_sp, b_sp):
    B = x_lat.shape[0]
    T = cos_t.shape[0]
    nt = T // TILE
    return pl.pallas_call(
        _inproj_kernel,
        grid=(B, nt),
        in_specs=[
            *_stream_specs(lat_first, 0),
            _mod_spec(0),
            _const_spec((1, D_MODEL)),
            _const_spec((D_MODEL, PROJ_WIDTH)),
            pl.BlockSpec((TILE, HEAD_DIM), lambda b, i: (i, 0)),
            pl.BlockSpec((TILE, HEAD_DIM), lambda b, i: (i, 0)),
            _const_spec((1, SGU_WIDTH)),
            _const_spec((SGU_WIDTH, SGU_WIDTH)),
            _const_spec((CHUNK, SGU_GROUPS * CHUNK)),
            _const_spec((CHUNK, SGU_WIDTH)),
        ],
        out_specs=[
            pl.BlockSpec((None, None, ATTN_WIDTH, TILE), lambda b, i: (b, i, 0, 0)),
            pl.BlockSpec((None, TILE, ATTN_WIDTH), lambda b, i: (b, i, 0)),
            pl.BlockSpec((None, None, ATTN_WIDTH, TILE), lambda b, i: (b, i, 0, 0)),
            pl.BlockSpec((None, TILE, 2 * LRU_WIDTH), lambda b, i: (b, i, 0)),
            pl.BlockSpec((None, TILE, SGU_WIDTH), lambda b, i: (b, i, 0)),
        ],
        out_shape=[
            jax.ShapeDtypeStruct((B, nt, ATTN_WIDTH, TILE), BF16),
            jax.ShapeDtypeStruct((B, T, ATTN_WIDTH), BF16),
            jax.ShapeDtypeStruct((B, nt, ATTN_WIDTH, TILE), BF16),
            jax.ShapeDtypeStruct((B, T, 2 * LRU_WIDTH), F32),
            jax.ShapeDtypeStruct((B, T, SGU_WIDTH), BF16),
        ],
        compiler_params=_cparams(("parallel", "parallel")),
        name="inproj",
    )(x_ctx, x_lat, mod, g1, w_in, cos_t, sin_t, g_sgu, ones_bd, w_sp, b_sp)


MAX_SCORE_GAP = 60.0

_ATTN_HEAD_SCRATCH = (
    [pltpu.VMEM((HEAD_DIM + BF16_ROWS, 2 * TILE), F32),
     pltpu.VMEM((1, 2 * TILE), F32),
     pltpu.VMEM((SUBLANES, 2 * TILE), F32),
     pltpu.VMEM((HEAD_DIM, 2 * TILE), BF16)]
    + 2 * [pltpu.VMEM((KEY_CHUNK, 2 * TILE), BF16)])


def _attn_kernel(q_ref, k_ref, v_ref, lq_ref, lk_ref, g_ref, o_ref, *scratch, lam_init, q_off):
    qi = pl.program_id(2) + q_off
    heads = range(ATTN_HEADS_PER_STEP)
    per_head = len(_ATTN_HEAD_SCRATCH)
    acc_ref, m_ref, smax_ref, rhs_ref, p_ref = [], [], [], [], []
    for h in heads:
        bufs = scratch[h * per_head:(h + 1) * per_head]
        acc_ref.append(bufs[0])
        m_ref.append(bufs[1])
        smax_ref.append(bufs[2])
        rhs_ref.append(bufs[3])
        p_ref.append(bufs[4:6])
    hsl = [slice(h * HEAD_DIM, (h + 1) * HEAD_DIM) for h in heads]
    row = lax.broadcasted_iota(jnp.int32, (HEAD_DIM, TILE), 0)
    for h in heads:
        qf = q_ref[hsl[h], :].astype(F32)
        rhs_ref[h][...] = jnp.concatenate([jnp.where(row < QK_DIM, qf, 0.0),
                                           jnp.where(row >= QK_DIM, qf, 0.0)],
                                          axis=1).astype(BF16)

    n_latent = (k_ref.shape[0] - CTX_LEN) // KEY_CHUNK

    def scores(h, c):
        if isinstance(c, int) and c == 0:
            rows = slice(0, CTX_LEN)
        else:
            rows = pl.ds(pl.multiple_of(CTX_LEN + (c - 1) * KEY_CHUNK, TILE), KEY_CHUNK)
        return jnp.dot(k_ref[rows, hsl[h]], rhs_ref[h][...], preferred_element_type=F32)

    def accumulate(h, c, p):
        ctx = isinstance(c, int) and c == 0
        v = (v_ref[0, hsl[h], :] if ctx else
             jnp.concatenate([v_ref[2 * c - 1, hsl[h], :], v_ref[2 * c, hsl[h], :]], axis=1))
        v = jnp.concatenate([v, jnp.ones((BF16_ROWS, v.shape[1]), BF16)], axis=0)
        upd = jnp.dot(v, p.astype(BF16), preferred_element_type=F32)
        acc_ref[h][...] = upd if ctx else acc_ref[h][...] + upd

    def probabilities(h, c):
        s = scores(h, c)
        if c == 0:
            m_ref[h][...] = jnp.max(s, axis=0, keepdims=True)
        else:
            smax = jnp.max(s.reshape(KEY_CHUNK // SUBLANES, SUBLANES, 2 * TILE), axis=0)
            smax_ref[h][...] = smax if c == 1 else jnp.maximum(smax_ref[h][...], smax)
        return jnp.exp2(s - m_ref[h][...]).astype(BF16)

    def stream(chunks):
        items = [(h, c) for c in chunks for h in heads]
        skew = 2
        ps = [probabilities(*item) for item in items[:skew]]
        for i, (h, c) in enumerate(items):
            if i + skew < len(items):
                ps.append(probabilities(*items[i + skew]))
            accumulate(h, c, ps[i])
            ps[i] = None

    def exact_softmax():
        for h in heads:
            m = jnp.maximum(m_ref[h][...], jnp.max(smax_ref[h][...], axis=0, keepdims=True))
            accumulate(h, 0, jnp.exp2(scores(h, 0) - m))

            def chunk(c, carry, h=h, m=m):
                accumulate(h, c, jnp.exp2(scores(h, c) - m))
                return carry

            lax.fori_loop(1, n_latent + 1, chunk, 0)

    def finish():
        t = jnp.sum(lq_ref[...] * lk_ref[...], axis=1, keepdims=True)
        e = jnp.exp(t)
        lam = e[0:1] - e[1:2] + lam_init
        for h in heads:
            acc = acc_ref[h][0:HEAD_DIM, :]
            inv = 1.0 / acc_ref[h][HEAD_DIM:HEAD_DIM + 1, :]
            ot = acc[:, :TILE] * inv[:, :TILE] - lam * (acc[:, TILE:] * inv[:, TILE:])
            rs = lax.rsqrt(jnp.mean(ot * ot, axis=0, keepdims=True) + EPS)
            y = (ot * rs).T * g_ref[...]
            o_ref[:, hsl[h]] = (y * (1.0 - lam_init)).astype(BF16)

    def context_queries():
        stream([0])
        finish()

    def latent_queries():
        stream(range(n_latent + 1))
        finish()
        gap = functools.reduce(
            jnp.maximum, [jnp.max(smax_ref[h][...] - m_ref[h][...]) for h in heads])

        @pl.when(gap > MAX_SCORE_GAP)
        def _():
            exact_softmax()
            finish()

    if q_off > 0:
        latent_queries()
    else:
        pl.when(qi == 0)(context_queries)
        pl.when(qi != 0)(latent_queries)


def _attention(qt, k, vt, lam_q, lam_k, g_attn, lam_init, q_off):
    B, nt, _, _ = qt.shape
    T = k.shape[1]
    hp = ATTN_HEADS_PER_STEP
    kern = functools.partial(_attn_kernel, lam_init=lam_init, q_off=q_off)
    return pl.pallas_call(
        kern,
        grid=(B, ATTN_HEADS // hp, nt - q_off),
        in_specs=[
            pl.BlockSpec((None, None, hp * HEAD_DIM, TILE), lambda b, h, i: (b, i + q_off, h, 0)),
            pl.BlockSpec((None, T, hp * HEAD_DIM), lambda b, h, i: (b, 0, h)),
            pl.BlockSpec((None, nt, hp * HEAD_DIM, TILE), lambda b, h, i: (b, 0, h, 0)),
            pl.BlockSpec((2, QK_DIM), lambda b, h, i: (0, 0)),
            pl.BlockSpec((2, QK_DIM), lambda b, h, i: (0, 0)),
            pl.BlockSpec((1, HEAD_DIM), lambda b, h, i: (0, 0)),
        ],
        out_specs=pl.BlockSpec((None, TILE, hp * HEAD_DIM), lambda b, h, i: (b, i, h)),
        out_shape=jax.ShapeDtypeStruct((B, (nt - q_off) * TILE, ATTN_WIDTH), BF16),
        scratch_shapes=[shape for _ in range(hp) for shape in _ATTN_HEAD_SCRATCH],
        compiler_params=_cparams(("parallel", "parallel", "arbitrary")),
        name="diff_attn",
    )(qt, k, vt, lam_q, lam_k, g_attn)


def _expm1(x, u):
    near = jnp.where(u == 1.0, x, (u - 1.0) * x / jnp.log(u))
    return jnp.where(jnp.abs(x) > 0.5, u - 1.0, near)


def _group_roll(x, shift):
    n, w = x.shape
    return pltpu.roll(x.reshape(n // SUBLANES, SUBLANES, w), shift % SUBLANES, 1).reshape(n, w)


def _chunk_scan(a, bt, carry, sub, reverse):
    n = a.shape[0]
    acc_a, acc_b = a, bt
    for k in (1, 2, 4):
        if reverse:
            valid = sub <= SUBLANES - 1 - k
            shift = n - k
        else:
            valid = sub >= k
            shift = k
        a_sh = jnp.where(valid, _group_roll(acc_a, shift), 1.0)
        b_sh = jnp.where(valid, _group_roll(acc_b, shift), 0.0)
        acc_b = acc_a * b_sh + acc_b
        acc_a = acc_a * a_sh
    groups = n // SUBLANES
    order = range(groups - 1, -1, -1) if reverse else range(groups)
    edge = 0 if reverse else SUBLANES - 1
    carries = [None] * groups
    c = carry
    for g in order:
        carries[g] = jnp.broadcast_to(c, (SUBLANES, a.shape[1]))
        r = g * SUBLANES + edge
        c = acc_a[r:r + 1] * c + acc_b[r:r + 1]
    h = acc_a * jnp.concatenate(carries, axis=0) + acc_b
    return h, c


def _lru_kernel(x_ref, cw_ref, cb_ref, wg_ref, bg_ref, lam_ref, o_ref, hs_ref, conv_ref):
    T = x_ref.shape[0]
    nt = T // TILE
    row = lax.broadcasted_iota(jnp.int32, (TILE, LRU_WIDTH), 0)
    sub = row & (SUBLANES - 1)
    cw = cw_ref[...]
    cb = cb_ref[...]

    def conv(j):
        r0 = pl.multiple_of(j * TILE, TILE)
        cur = x_ref[pl.ds(r0, TILE), 0:LRU_WIDTH]
        pr = pl.multiple_of(jnp.maximum(r0 - SUBLANES, 0), SUBLANES)
        nx = pl.multiple_of(jnp.minimum(r0 + TILE, T - SUBLANES), SUBLANES)
        prev8 = x_ref[pl.ds(pr, SUBLANES), 0:LRU_WIDTH]
        next8 = x_ref[pl.ds(nx, SUBLANES), 0:LRU_WIDTH]
        has_prev = j >= 2
        has_next = jnp.logical_and(j >= 1, j <= nt - 2)
        p1 = jnp.where(has_prev, prev8[SUBLANES - 1:SUBLANES], 0.0)
        n0 = jnp.where(has_next, next8[0:1], 0.0)
        n1 = jnp.where(has_next, next8[1:2], 0.0)
        xm1 = jnp.where(row == 0, p1, pltpu.roll(cur, 1, 0))
        xp1 = jnp.where(row == TILE - 1, n0, pltpu.roll(cur, TILE - 1, 0))
        xp2 = jnp.where(row == TILE - 2, n0,
                        jnp.where(row == TILE - 1, n1, pltpu.roll(cur, TILE - 2, 0)))
        return cw[0:1] * xm1 + cw[1:2] * cur + cw[2:3] * xp1 + cw[3:4] * xp2 + cb

    def direction(j, d, carry):
        rows = pl.ds(pl.multiple_of(j * TILE, TILE), TILE)
        if d == 0:
            c = conv(j)
            conv_ref[rows, :] = c
        else:
            c = conv_ref[rows, :]
        gates = jnp.dot(c.astype(BF16), wg_ref[d], preferred_element_type=F32) + bg_ref[d]
        r = jax.nn.sigmoid(gates[:, :LRU_WIDTH])
        i = jax.nn.sigmoid(gates[:, LRU_WIDTH:])
        log_a = (-LRU_C * r) * jax.nn.softplus(-lam_ref[d])
        a = jnp.exp(log_a)
        bt = jnp.sqrt(jnp.maximum(-_expm1(2.0 * log_a, a * a), 0.0)) * (i * c)
        return _chunk_scan(a, bt, carry, sub, reverse=(d == 1))

    def fwd(j, carry):
        h, carry = direction(j, 0, carry)
        hs_ref[pl.ds(pl.multiple_of(j * TILE, TILE), TILE), :] = h
        return carry

    def rev(n, carry):
        j = jnp.where(n == 0, 0, nt - n)
        h, carry = direction(j, 1, carry)
        r0 = pl.multiple_of(j * TILE, TILE)
        gate = x_ref[pl.ds(r0, TILE), LRU_WIDTH:]
        o_ref[pl.ds(r0, TILE), :] = ((hs_ref[pl.ds(r0, TILE), :] + h) * gate).astype(BF16)
        return carry

    zero = jnp.zeros((1, LRU_WIDTH), F32)
    lax.fori_loop(0, nt, fwd, zero)
    lax.fori_loop(0, nt, rev, zero)


def _lru(lru_in, conv_w, conv_b, w_gate, b_gate, lam):
    B, T, _ = lru_in.shape
    return pl.pallas_call(
        _lru_kernel,
        grid=(B,),
        in_specs=[
            pl.BlockSpec((None, T, 2 * LRU_WIDTH), lambda b: (b, 0, 0)),
            _const_spec((CONV_WIDTH, LRU_WIDTH)),
            _const_spec((1, LRU_WIDTH)),
            _const_spec((2, LRU_WIDTH, 2 * LRU_WIDTH)),
            _const_spec((2, 1, 2 * LRU_WIDTH)),
            _const_spec((2, 1, LRU_WIDTH)),
        ],
        out_specs=pl.BlockSpec((None, T, LRU_WIDTH), lambda b: (b, 0, 0)),
        out_shape=jax.ShapeDtypeStruct((B, T, LRU_WIDTH), BF16),
        scratch_shapes=[pltpu.VMEM((T, LRU_WIDTH), F32),
                        pltpu.VMEM((T, LRU_WIDTH), F32)],
        compiler_params=_cparams(("parallel",)),
        name="rglru",
    )(lru_in, conv_w, conv_b, w_gate, b_gate, lam)


def _ffn_kernel(ya_ref, yr_ref, ys_ref, xc_ref, xl_ref, mod_ref, g2_ref, gf_ref, wo_ref, wg_ref,
                wu_ref, wd_ref, o_ref, *, t_off, final):
    y = jnp.concatenate([ya_ref[...], yr_ref[...], ys_ref[...]], axis=-1)
    x = _stream_tile(xc_ref, xl_ref, t_off)
    x1 = x + mod_ref[2] * jnp.dot(y, wo_ref[...], preferred_element_type=F32)
    rs = lax.rsqrt(jnp.mean(x1 * x1, axis=-1, keepdims=True) + EPS)
    h = ((x1 * rs * g2_ref[...]) * (1.0 + mod_ref[4]) + mod_ref[3]).astype(BF16)
    g = jnp.dot(h, wg_ref[...], preferred_element_type=F32)
    u = jnp.dot(h, wu_ref[...], preferred_element_type=F32)
    act = ((g * jax.nn.sigmoid(g)) * u).astype(BF16)
    x2 = x1 + mod_ref[5] * jnp.dot(act, wd_ref[...], preferred_element_type=F32)
    if final:
        x2 = x2 * lax.rsqrt(jnp.mean(x2 * x2, axis=-1, keepdims=True) + EPS) * gf_ref[...]
    o_ref[...] = x2


def _ffn(ya, yr, ys, x_ctx, x_lat, lat_first, mod, g2, g_final, w_out, w_gate, w_up, w_down,
         t_off, final):
    B, T, _ = yr.shape
    nt = T // TILE - t_off
    kern = functools.partial(_ffn_kernel, t_off=t_off, final=final)
    tok = lambda w, off=t_off: pl.BlockSpec((None, TILE, w), lambda b, i: (b, i + off, 0))
    ya_off = t_off - (T - ya.shape[1]) // TILE
    return pl.pallas_call(
        kern,
        grid=(B, nt),
        in_specs=[
            tok(ATTN_WIDTH, ya_off), tok(LRU_WIDTH), tok(SGU_WIDTH),
            *_stream_specs(lat_first, t_off),
            _mod_spec(t_off),
            _const_spec((1, D_MODEL)),
            _const_spec((1, D_MODEL)),
            _const_spec((D_MODEL, D_MODEL)),
            _const_spec((D_MODEL, FFN_HIDDEN)),
            _const_spec((D_MODEL, FFN_HIDDEN)),
            _const_spec((FFN_HIDDEN, D_MODEL)),
        ],
        out_specs=pl.BlockSpec((None, TILE, D_MODEL), lambda b, i: (b, i, 0)),
        out_shape=jax.ShapeDtypeStruct((B, nt * TILE, D_MODEL), F32),
        compiler_params=_cparams(("parallel", "parallel")),
        name="outproj_ffn",
    )(ya, yr, ys, x_ctx, x_lat, mod, g2, g_final, w_out, w_gate, w_up, w_down)


def _rope_tables(seq):
    rows = seq // GRID_W
    row = jnp.repeat(jnp.arange(rows), GRID_W).astype(F32)
    col = jnp.tile(jnp.arange(GRID_W), rows).astype(F32)
    n = ROPE_AXIS_DIM // 2
    inv = ROPE_THETA ** (-jnp.arange(n, dtype=F32) / n)
    ang_r = row[:, None] * inv[None, :]
    ang_c = col[:, None] * inv[None, :]
    cr, sr, cc, sc = jnp.cos(ang_r), jnp.sin(ang_r), jnp.cos(ang_c), jnp.sin(ang_c)
    cos = jnp.tile(jnp.concatenate([cr, cr, cc, cc], axis=1), (1, 2))
    sin = jnp.tile(jnp.concatenate([-sr, sr, -sc, sc], axis=1), (1, 2))
    cos = jnp.concatenate([jnp.ones((CTX_LEN, HEAD_DIM), F32), cos], axis=0)
    sin = jnp.concatenate([jnp.zeros((CTX_LEN, HEAD_DIM), F32), sin], axis=0)
    return cos, sin


def _block_diag(w):
    eye = jnp.eye(w.shape[0], dtype=w.dtype)
    return jnp.einsum('hij,hk->hikj', w, eye).reshape(w.shape[0] * w.shape[1],
                                                      w.shape[0] * w.shape[2])


def kernel(x, c, ctx, c_ctx, w_ada, b_ada, g_norm1, g_norm2, w_in, lam_q, lam_k, g_attn, conv_w,
           conv_b, w_rg_a, b_rg_a, w_rg_x, b_rg_x, lru_lambda, g_sgu, w_spatial, b_spatial, w_out,
           w_ffn_gate, w_ffn_up, w_ffn_down, g_final):
    B, S, D = x.shape
    assert (D, ctx.shape[1], S % KEY_CHUNK) == (D_MODEL, CTX_LEN, 0) and B <= 8
    cos_t, sin_t = _rope_tables(S)
    cs = jnp.zeros((16, D), F32).at[:B].set(c).at[8].set(c_ctx)
    mods = _ada(cs, w_ada, b_ada)
    ones_bd = _block_diag(jnp.ones((SGU_GROUPS, SGU_GROUP_DIM, SGU_GROUP_DIM), BF16))
    stream = (ctx, x, 1)

    for l in range(DEPTH):
        last = l == DEPTH - 1
        lam_init = 0.8 - 0.6 * math.exp(-0.3 * l)
        mod = mods[l].reshape(16, N_MOD, 1, D)
        w_sp = jnp.transpose(w_spatial[l], (1, 0, 2)).reshape(CHUNK, SGU_GROUPS * CHUNK)
        b_sp = jnp.repeat(b_spatial[l].T, SGU_GROUP_DIM, axis=1)
        w_gate = jnp.stack([jnp.concatenate([_block_diag(w_rg_a[l, d]), _block_diag(w_rg_x[l, d])],
                                            axis=1) for d in range(2)]).astype(BF16)
        b_gate = jnp.concatenate([b_rg_a[l], b_rg_x[l]], axis=1)[:, None, :]

        qt, k, vt, lru_in, ys = _inproj(
            *stream, mod, g_norm1[l][None], w_in[l].astype(BF16), cos_t, sin_t, g_sgu[l][None],
            ones_bd, w_sp.astype(BF16), b_sp)
        t_off = 1 if last else 0
        ya = _attention(qt, k, vt, lam_q[l], lam_k[l], g_attn[l][None], lam_init, t_off)
        yr = _lru(lru_in, conv_w[l], conv_b[l][None], w_gate, b_gate, lru_lambda[l][:, None, :])
        xs = _ffn(ya, yr, ys, *stream, mod, g_norm2[l][None], g_final[None],
                  w_out[l].astype(BF16), w_ffn_gate[l].astype(BF16), w_ffn_up[l].astype(BF16),
                  w_ffn_down[l].astype(BF16), t_off, last)
        stream = (xs, xs, 0)
    return xs
```

```python
import functools
import math

import jax
import jax.numpy as jnp
from jax import lax
from jax.experimental import pallas as pl
from jax.experimental.pallas import tpu as pltpu

F32 = jnp.float32
BF16 = jnp.bfloat16

D_MODEL = 1024
DEPTH = 2
GRID_W = 64
CTX_LEN = 256
N_MOD = 6
EPS = 1e-6
ATTN_WIDTH = 512
LRU_WIDTH = 256
SGU_WIDTH = 256
ATTN_HEADS = 4
ATTN_HEADS_PER_STEP = 4
HEAD_DIM = 128
QK_DIM = 64
ROPE_AXIS_DIM = 32
ROPE_THETA = 10000.0
LRU_HEADS = 4
LRU_BLOCK = 64
CONV_WIDTH = 4
LRU_C = 8.0
SGU_GROUPS = 4
SGU_GROUP_DIM = 64
CHUNK = 128
FFN_HIDDEN = 2816
PROJ_WIDTH = 2560

TILE = 256
SUBTILES = 2
KEY_CHUNK = 2 * TILE
SUBLANES = 8
VMEM_LIMIT = 56 * 1024 * 1024


def _cparams(sem, flags=None):
    return pltpu.CompilerParams(dimension_semantics=sem, vmem_limit_bytes=VMEM_LIMIT, flags=flags)


def _const_spec(shape):
    nd = len(shape)
    return pl.BlockSpec(shape, lambda *_: (0,) * nd, pipeline_mode=pl.Buffered(1))


def _sub_tile(i, j, t_off, n_tiles):
    return jnp.minimum(SUBTILES * i + j + t_off, n_tiles - 1)


def _token_spec(tile, width, first=0):
    return pl.BlockSpec((None, TILE, width), lambda b, i: (b, tile(i) - first, 0))


def _latent_spec(tile, lat_first):
    return pl.BlockSpec((None, TILE, D_MODEL),
                        lambda b, i: (b, jnp.maximum(tile(i), 1) - lat_first, 0))


def _mod_spec(tile):
    return pl.BlockSpec((None, N_MOD, 1, D_MODEL),
                        lambda b, i: (jnp.where(tile(i) == 0, 8, b), 0, 0, 0))


_CTX_SPEC = pl.BlockSpec((None, TILE, D_MODEL), lambda b, i: (b, 0, 0))


def _ada_kernel(c_ref, w_ref, b_ref, o_ref):
    c = c_ref[...]
    h = (c * jax.nn.sigmoid(c)).astype(BF16)
    o_ref[...] = jnp.dot(h, w_ref[...].astype(BF16), preferred_element_type=F32) + b_ref[...]


def _ada(cs, w_ada, b_ada):
    tn = 1536
    n_out = N_MOD * D_MODEL
    return pl.pallas_call(
        _ada_kernel,
        grid=(DEPTH, n_out // tn),
        in_specs=[
            pl.BlockSpec((16, D_MODEL), lambda l, n: (0, 0)),
            pl.BlockSpec((None, D_MODEL, tn), lambda l, n: (l, 0, n)),
            pl.BlockSpec((None, 1, tn), lambda l, n: (l, 0, n)),
        ],
        out_specs=pl.BlockSpec((None, 16, tn), lambda l, n: (l, 0, n)),
        out_shape=jax.ShapeDtypeStruct((DEPTH, 16, n_out), F32),
        compiler_params=_cparams(("arbitrary", "arbitrary")),
        name="ada_mod",
    )(cs, w_ada, b_ada.reshape(DEPTH, 1, n_out))


def _rope(x, cos, sin, first_half):
    partner = jnp.where(first_half, pltpu.roll(x, 128 - 16, 1), pltpu.roll(x, 16, 1))
    return x * cos + partner * sin


def _inproj_kernel(*refs):
    per_sub = 4
    subs = [refs[per_sub * j:per_sub * (j + 1)] for j in range(SUBTILES)]
    (xc_ref, g1_ref, w_ref, gs_ref, ones_ref, wsp_ref, bsp_ref,
     qt_ref, k_ref, vt_ref, lru_ref, sgu_ref) = refs[per_sub * SUBTILES:]
    first_tile = SUBTILES * pl.program_id(1)

    p_all = []
    for j, (xl_ref, mod_ref, _, _) in enumerate(subs):
        x = jnp.where(first_tile + j == 0, xc_ref[...], xl_ref[...])
        rs = lax.rsqrt(jnp.mean(x * x, axis=-1, keepdims=True) + EPS)
        h = (x * rs * g1_ref[...]) * (1.0 + mod_ref[1]) + mod_ref[0]
        p_all.append(jnp.dot(h.astype(BF16), w_ref[...], preferred_element_type=F32))

    lane = lax.broadcasted_iota(jnp.int32, (TILE, HEAD_DIM), 1)
    first_half = (lane & 16) == 0
    grp = lax.broadcasted_iota(jnp.int32, (CHUNK, SGU_WIDTH), 1) // SGU_GROUP_DIM
    scale = QK_DIM ** -0.5 * math.log2(math.e)
    for j, (_, _, cos_ref, sin_ref) in enumerate(subs):
        p = p_all[j]
        rows = slice(j * TILE, (j + 1) * TILE)
        cos = cos_ref[...]
        sin = sin_ref[...]
        q_parts, k_parts = [], []
        for hd in range(ATTN_HEADS):
            q = p[:, hd * HEAD_DIM:(hd + 1) * HEAD_DIM]
            k = p[:, ATTN_WIDTH + hd * HEAD_DIM:ATTN_WIDTH + (hd + 1) * HEAD_DIM]
            q_parts.append(_rope(q, cos, sin, first_half) * scale)
            k_parts.append(_rope(k, cos, sin, first_half))
        q = jnp.concatenate(q_parts, axis=1)
        qt_ref[j] = q.T.astype(BF16)
        k_ref[rows, :] = jnp.concatenate(k_parts, axis=1).astype(BF16)
        vt_ref[j] = p[:, 2 * ATTN_WIDTH:3 * ATTN_WIDTH].T.astype(BF16)

        lx0 = 3 * ATTN_WIDTH
        lru_ref[rows, 0:LRU_WIDTH] = p[:, lx0:lx0 + LRU_WIDTH]
        lru_ref[rows, LRU_WIDTH:] = jax.nn.gelu(p[:, lx0 + LRU_WIDTH:lx0 + 2 * LRU_WIDTH])

        su0 = lx0 + 2 * LRU_WIDTH
        u = jax.nn.gelu(p[:, su0:su0 + SGU_WIDTH])
        v = jax.nn.gelu(p[:, su0 + SGU_WIDTH:su0 + 2 * SGU_WIDTH])
        ss = v * v
        hi = ss.astype(BF16)
        lo = (ss - hi.astype(F32)).astype(BF16)
        gsum = (jnp.dot(hi, ones_ref[...], preferred_element_type=F32)
                + jnp.dot(lo, ones_ref[...], preferred_element_type=F32))
        vn = v * lax.rsqrt(gsum * (1.0 / SGU_GROUP_DIM) + EPS) * gs_ref[...]
        for c in range(TILE // CHUNK):
            vc = vn[c * CHUNK:(c + 1) * CHUNK]
            rhs = jnp.concatenate(
                [jnp.where(grp == g, vc, 0.0) for g in range(SGU_GROUPS)], axis=0).astype(BF16)
            m = jnp.dot(wsp_ref[...], rhs, preferred_element_type=F32) + bsp_ref[...]
            r0 = j * TILE + c * CHUNK
            sgu_ref[r0:r0 + CHUNK, :] = (u[c * CHUNK:(c + 1) * CHUNK] * m).astype(BF16)


def _inproj(x_ctx, x_lat, lat_first, mod, g1, w_in, cos_t, sin_t, g_sgu, ones_bd, w_sp, b_sp):
    B = x_lat.shape[0]
    T = cos_t.shape[0]
    nt = T // TILE

    def sub_specs(j):
        tile = lambda i: _sub_tile(i, j, 0, nt)
        table = pl.BlockSpec((TILE, HEAD_DIM), lambda b, i: (tile(i), 0))
        return [_latent_spec(tile, lat_first), _mod_spec(tile), table, table]

    tokens = []
    for j in range(SUBTILES):
        tokens += [x_lat, mod, cos_t, sin_t]
    rows = SUBTILES * TILE
    return pl.pallas_call(
        _inproj_kernel,
        grid=(B, pl.cdiv(nt, SUBTILES)),
        in_specs=[
            *[spec for j in range(SUBTILES) for spec in sub_specs(j)],
            _CTX_SPEC,
            _const_spec((1, D_MODEL)),
            _const_spec((D_MODEL, PROJ_WIDTH)),
            _const_spec((1, SGU_WIDTH)),
            _const_spec((SGU_WIDTH, SGU_WIDTH)),
            _const_spec((CHUNK, SGU_GROUPS * CHUNK)),
            _const_spec((CHUNK, SGU_WIDTH)),
        ],
        out_specs=[
            pl.BlockSpec((None, SUBTILES, ATTN_WIDTH, TILE), lambda b, i: (b, i, 0, 0)),
            pl.BlockSpec((None, rows, ATTN_WIDTH), lambda b, i: (b, i, 0)),
            pl.BlockSpec((None, SUBTILES, ATTN_WIDTH, TILE), lambda b, i: (b, i, 0, 0)),
            pl.BlockSpec((None, rows, 2 * LRU_WIDTH), lambda b, i: (b, i, 0)),
            pl.BlockSpec((None, rows, SGU_WIDTH), lambda b, i: (b, i, 0)),
        ],
        out_shape=[
            jax.ShapeDtypeStruct((B, nt, ATTN_WIDTH, TILE), BF16),
            jax.ShapeDtypeStruct((B, T, ATTN_WIDTH), BF16),
            jax.ShapeDtypeStruct((B, nt, ATTN_WIDTH, TILE), BF16),
            jax.ShapeDtypeStruct((B, T, 2 * LRU_WIDTH), F32),
            jax.ShapeDtypeStruct((B, T, SGU_WIDTH), BF16),
        ],
        compiler_params=_cparams(("parallel", "parallel")),
        name="inproj",
    )(*tokens, x_ctx, g1, w_in, g_sgu, ones_bd, w_sp, b_sp)


MAX_SCORE_GAP = 60.0

_ATTN_HEAD_SCRATCH = [
    pltpu.VMEM((HEAD_DIM, 2 * TILE), F32),
    pltpu.VMEM((SUBLANES, 2 * TILE), F32),
    pltpu.VMEM((1, 2 * TILE), F32),
    pltpu.VMEM((SUBLANES, 2 * TILE), F32),
    pltpu.VMEM((HEAD_DIM, 2 * TILE), BF16)]


def _attn_kernel(q_ref, k_ref, v_ref, lq_ref, lk_ref, g_ref, o_ref, *scratch, lam_init, q_off):
    qi = pl.program_id(2) + q_off
    heads = range(ATTN_HEADS_PER_STEP)
    per_head = len(_ATTN_HEAD_SCRATCH)
    acc_ref, den_ref, m_ref, smax_ref, rhs_ref = [scratch[n::per_head] for n in range(per_head)]
    hsl = [slice(h * HEAD_DIM, (h + 1) * HEAD_DIM) for h in heads]
    row = lax.broadcasted_iota(jnp.int32, (HEAD_DIM, TILE), 0)
    for h in heads:
        qf = q_ref[hsl[h], :].astype(F32)
        rhs_ref[h][...] = jnp.concatenate([jnp.where(row < QK_DIM, qf, 0.0),
                                           jnp.where(row >= QK_DIM, qf, 0.0)],
                                          axis=1).astype(BF16)

    n_latent = (k_ref.shape[0] - CTX_LEN) // KEY_CHUNK

    def scores(h, c):
        if isinstance(c, int) and c == 0:
            rows = slice(0, CTX_LEN)
        else:
            rows = pl.ds(pl.multiple_of(CTX_LEN + (c - 1) * KEY_CHUNK, TILE), KEY_CHUNK)
        return jnp.dot(k_ref[rows, hsl[h]], rhs_ref[h][...], preferred_element_type=F32)

    def by_sublane(x):
        return x.reshape(x.shape[0] // SUBLANES, SUBLANES, x.shape[1])

    def exponentiate(h, c, s, m):
        ctx = isinstance(c, int) and c == 0
        p = jnp.exp2(s - m)
        psum = jnp.sum(by_sublane(p), axis=0)
        den_ref[h][...] = psum if ctx else den_ref[h][...] + psum
        return p.astype(BF16)

    def accumulate(h, c, p):
        ctx = isinstance(c, int) and c == 0
        v = (v_ref[0, hsl[h], :] if ctx else
             jnp.concatenate([v_ref[2 * c - 1, hsl[h], :], v_ref[2 * c, hsl[h], :]], axis=1))
        upd = jnp.dot(v, p, preferred_element_type=F32)
        acc_ref[h][...] = upd if ctx else acc_ref[h][...] + upd

    def probabilities(h, c):
        s = scores(h, c)
        if c == 0:
            m_ref[h][...] = jnp.max(s, axis=0, keepdims=True)
        else:
            smax = jnp.max(by_sublane(s), axis=0)
            smax_ref[h][...] = smax if c == 1 else jnp.maximum(smax_ref[h][...], smax)
        return exponentiate(h, c, s, m_ref[h][...])

    def stream(chunks):
        items = [(h, c) for c in chunks for h in heads]
        skew = 2
        ps = [probabilities(*item) for item in items[:skew]]
        for i, (h, c) in enumerate(items):
            if i + skew < len(items):
                ps.append(probabilities(*items[i + skew]))
            accumulate(h, c, ps[i])
            ps[i] = None

    def exact_softmax():
        for h in heads:
            m = jnp.maximum(m_ref[h][...], jnp.max(smax_ref[h][...], axis=0, keepdims=True))
            accumulate(h, 0, exponentiate(h, 0, scores(h, 0), m))

            def chunk(c, carry, h=h, m=m):
                accumulate(h, c, exponentiate(h, c, scores(h, c), m))
                return carry

            lax.fori_loop(1, n_latent + 1, chunk, 0)

    def finish():
        t = jnp.sum(lq_ref[...] * lk_ref[...], axis=1, keepdims=True)
        e = jnp.exp(t)
        lam = e[0:1] - e[1:2] + lam_init
        for h in heads:
            acc = acc_ref[h][...]
            inv = 1.0 / jnp.sum(den_ref[h][...], axis=0, keepdims=True)
            ot = acc[:, :TILE] * inv[:, :TILE] - lam * (acc[:, TILE:] * inv[:, TILE:])
            rs = lax.rsqrt(jnp.mean(ot * ot, axis=0, keepdims=True) + EPS)
            y = (ot * rs).T * g_ref[...]
            o_ref[:, hsl[h]] = (y * (1.0 - lam_init)).astype(BF16)

    def context_queries():
        stream([0])
        finish()

    def latent_queries():
        stream(range(n_latent + 1))
        finish()
        gap = functools.reduce(
            jnp.maximum, [jnp.max(smax_ref[h][...] - m_ref[h][...]) for h in heads])

        @pl.when(gap > MAX_SCORE_GAP)
        def _():
            exact_softmax()
            finish()

    if q_off > 0:
        latent_queries()
    else:
        pl.when(qi == 0)(context_queries)
        pl.when(qi != 0)(latent_queries)


def _attention(qt, k, vt, lam_q, lam_k, g_attn, lam_init, q_off):
    B, nt, _, _ = qt.shape
    T = k.shape[1]
    hp = ATTN_HEADS_PER_STEP
    kern = functools.partial(_attn_kernel, lam_init=lam_init, q_off=q_off)
    return pl.pallas_call(
        kern,
        grid=(B, ATTN_HEADS // hp, nt - q_off),
        in_specs=[
            pl.BlockSpec((None, None, hp * HEAD_DIM, TILE), lambda b, h, i: (b, i + q_off, h, 0)),
            pl.BlockSpec((None, T, hp * HEAD_DIM), lambda b, h, i: (b, 0, h)),
            pl.BlockSpec((None, nt, hp * HEAD_DIM, TILE), lambda b, h, i: (b, 0, h, 0)),
            pl.BlockSpec((2, QK_DIM), lambda b, h, i: (0, 0)),
            pl.BlockSpec((2, QK_DIM), lambda b, h, i: (0, 0)),
            pl.BlockSpec((1, HEAD_DIM), lambda b, h, i: (0, 0)),
        ],
        out_specs=pl.BlockSpec((None, TILE, hp * HEAD_DIM), lambda b, h, i: (b, i, h)),
        out_shape=jax.ShapeDtypeStruct((B, (nt - q_off) * TILE, ATTN_WIDTH), BF16),
        scratch_shapes=[shape for _ in range(hp) for shape in _ATTN_HEAD_SCRATCH],
        compiler_params=_cparams(("parallel", "parallel", "arbitrary")),
        name="diff_attn",
    )(qt, k, vt, lam_q, lam_k, g_attn)


def _expm1(x, u):
    near = jnp.where(u == 1.0, x, (u - 1.0) * x / jnp.log(u))
    return jnp.where(jnp.abs(x) > 0.5, u - 1.0, near)


def _group_roll(x, shift):
    n, w = x.shape
    return pltpu.roll(x.reshape(n // SUBLANES, SUBLANES, w), shift % SUBLANES, 1).reshape(n, w)


def _chunk_scan(a, bt, carry, sub, reverse):
    n = a.shape[0]
    acc_a, acc_b = a, bt
    for k in (1, 2, 4):
        if reverse:
            valid = sub <= SUBLANES - 1 - k
            shift = n - k
        else:
            valid = sub >= k
            shift = k
        a_sh = jnp.where(valid, _group_roll(acc_a, shift), 1.0)
        b_sh = jnp.where(valid, _group_roll(acc_b, shift), 0.0)
        acc_b = acc_a * b_sh + acc_b
        acc_a = acc_a * a_sh
    groups = n // SUBLANES
    order = range(groups - 1, -1, -1) if reverse else range(groups)
    edge = 0 if reverse else SUBLANES - 1
    carries = [None] * groups
    c = carry
    for g in order:
        carries[g] = jnp.broadcast_to(c, (SUBLANES, a.shape[1]))
        r = g * SUBLANES + edge
        c = acc_a[r:r + 1] * c + acc_b[r:r + 1]
    h = acc_a * jnp.concatenate(carries, axis=0) + acc_b
    return h, c


def _lru_kernel(x_ref, cw_ref, cb_ref, wg_ref, bg_ref, lam_ref, o_ref, hs_ref, conv_ref):
    T = x_ref.shape[0]
    nt = T // TILE
    row = lax.broadcasted_iota(jnp.int32, (TILE, LRU_WIDTH), 0)
    sub = row & (SUBLANES - 1)
    cw = cw_ref[...]
    cb = cb_ref[...]

    def conv(j):
        r0 = pl.multiple_of(j * TILE, TILE)
        cur = x_ref[pl.ds(r0, TILE), 0:LRU_WIDTH]
        pr = pl.multiple_of(jnp.maximum(r0 - SUBLANES, 0), SUBLANES)
        nx = pl.multiple_of(jnp.minimum(r0 + TILE, T - SUBLANES), SUBLANES)
        prev8 = x_ref[pl.ds(pr, SUBLANES), 0:LRU_WIDTH]
        next8 = x_ref[pl.ds(nx, SUBLANES), 0:LRU_WIDTH]
        has_prev = j >= 2
        has_next = jnp.logical_and(j >= 1, j <= nt - 2)
        p1 = jnp.where(has_prev, prev8[SUBLANES - 1:SUBLANES], 0.0)
        n0 = jnp.where(has_next, next8[0:1], 0.0)
        n1 = jnp.where(has_next, next8[1:2], 0.0)
        xm1 = jnp.where(row == 0, p1, pltpu.roll(cur, 1, 0))
        xp1 = jnp.where(row == TILE - 1, n0, pltpu.roll(cur, TILE - 1, 0))
        xp2 = jnp.where(row == TILE - 2, n0,
                        jnp.where(row == TILE - 1, n1, pltpu.roll(cur, TILE - 2, 0)))
        return cw[0:1] * xm1 + cw[1:2] * cur + cw[2:3] * xp1 + cw[3:4] * xp2 + cb

    def direction(j, d, carry):
        rows = pl.ds(pl.multiple_of(j * TILE, TILE), TILE)
        if d == 0:
            c = conv(j)
            conv_ref[rows, :] = c
        else:
            c = conv_ref[rows, :]
        gates = jnp.dot(c.astype(BF16), wg_ref[d], preferred_element_type=F32) + bg_ref[d]
        r = jax.nn.sigmoid(gates[:, :LRU_WIDTH])
        i = jax.nn.sigmoid(gates[:, LRU_WIDTH:])
        log_a = (-LRU_C * r) * jax.nn.softplus(-lam_ref[d])
        a = jnp.exp(log_a)
        y = -_expm1(2.0 * log_a, a * a)
        root = jnp.where(y > 0.0, y * lax.rsqrt(y), 0.0)
        bt = root * (i * c)
        return _chunk_scan(a, bt, carry, sub, reverse=(d == 1))

    def fwd(j, carry):
        h, carry = direction(j, 0, carry)
        hs_ref[pl.ds(pl.multiple_of(j * TILE, TILE), TILE), :] = h
        return carry

    def rev(n, carry):
        j = jnp.where(n == 0, 0, nt - n)
        h, carry = direction(j, 1, carry)
        r0 = pl.multiple_of(j * TILE, TILE)
        gate = x_ref[pl.ds(r0, TILE), LRU_WIDTH:]
        o_ref[pl.ds(r0, TILE), :] = ((hs_ref[pl.ds(r0, TILE), :] + h) * gate).astype(BF16)
        return carry

    zero = jnp.zeros((1, LRU_WIDTH), F32)
    lax.fori_loop(0, nt, fwd, zero)
    lax.fori_loop(0, nt, rev, zero)


def _lru(lru_in, conv_w, conv_b, w_gate, b_gate, lam):
    B, T, _ = lru_in.shape
    return pl.pallas_call(
        _lru_kernel,
        grid=(B,),
        in_specs=[
            pl.BlockSpec((None, T, 2 * LRU_WIDTH), lambda b: (b, 0, 0)),
            _const_spec((CONV_WIDTH, LRU_WIDTH)),
            _const_spec((1, LRU_WIDTH)),
            _const_spec((2, LRU_WIDTH, 2 * LRU_WIDTH)),
            _const_spec((2, 1, 2 * LRU_WIDTH)),
            _const_spec((2, 1, LRU_WIDTH)),
        ],
        out_specs=pl.BlockSpec((None, T, LRU_WIDTH), lambda b: (b, 0, 0)),
        out_shape=jax.ShapeDtypeStruct((B, T, LRU_WIDTH), BF16),
        scratch_shapes=[pltpu.VMEM((T, LRU_WIDTH), F32),
                        pltpu.VMEM((T, LRU_WIDTH), F32)],
        compiler_params=_cparams(("parallel",)),
        name="rglru",
    )(lru_in, conv_w, conv_b, w_gate, b_gate, lam)


def _ffn_kernel(*refs, t_off, final):
    per_sub = 5
    subs = [refs[per_sub * j:per_sub * (j + 1)] for j in range(SUBTILES)]
    xc_ref, g2_ref, gf_ref, wo_ref, wg_ref, wu_ref, wd_ref, o_ref = refs[per_sub * SUBTILES:]
    first_tile = SUBTILES * pl.program_id(1) + t_off

    proj = [jnp.dot(jnp.concatenate([ya[...], yr[...], ys[...]], axis=-1), wo_ref[...],
                    preferred_element_type=F32) for ya, yr, ys, _, _ in subs]
    x1, g, u = [], [], []
    for j, (_, _, _, xl_ref, mod_ref) in enumerate(subs):
        x = jnp.where(first_tile + j == 0, xc_ref[...], xl_ref[...])
        x1.append(x + mod_ref[2] * proj[j])
        rs = lax.rsqrt(jnp.mean(x1[j] * x1[j], axis=-1, keepdims=True) + EPS)
        h = ((x1[j] * rs * g2_ref[...]) * (1.0 + mod_ref[4]) + mod_ref[3]).astype(BF16)
        g.append(jnp.dot(h, wg_ref[...], preferred_element_type=F32))
        u.append(jnp.dot(h, wu_ref[...], preferred_element_type=F32))
    for j, (_, _, _, _, mod_ref) in enumerate(subs):
        act = ((g[j] * jax.nn.sigmoid(g[j])) * u[j]).astype(BF16)
        x2 = x1[j] + mod_ref[5] * jnp.dot(act, wd_ref[...], preferred_element_type=F32)
        if final:
            x2 = x2 * lax.rsqrt(jnp.mean(x2 * x2, axis=-1, keepdims=True) + EPS) * gf_ref[...]
        o_ref[j * TILE:(j + 1) * TILE, :] = x2


def _ffn(ya, yr, ys, x_ctx, x_lat, lat_first, mod, g2, g_final, w_out, w_gate, w_up, w_down,
         t_off, final):
    B, T, _ = yr.shape
    n_tiles = T // TILE
    nt = n_tiles - t_off
    ya_first = (T - ya.shape[1]) // TILE
    kern = functools.partial(_ffn_kernel, t_off=t_off, final=final)

    def sub_specs(j):
        tile = lambda i: _sub_tile(i, j, t_off, n_tiles)
        return [_token_spec(tile, ATTN_WIDTH, ya_first), _token_spec(tile, LRU_WIDTH),
                _token_spec(tile, SGU_WIDTH), _latent_spec(tile, lat_first), _mod_spec(tile)]

    tokens = []
    for j in range(SUBTILES):
        tokens += [ya, yr, ys, x_lat, mod]
    return pl.pallas_call(
        kern,
        grid=(B, pl.cdiv(nt, SUBTILES)),
        in_specs=[
            *[spec for j in range(SUBTILES) for spec in sub_specs(j)],
            _CTX_SPEC,
            _const_spec((1, D_MODEL)),
            _const_spec((1, D_MODEL)),
            _const_spec((D_MODEL, D_MODEL)),
            _const_spec((D_MODEL, FFN_HIDDEN)),
            _const_spec((D_MODEL, FFN_HIDDEN)),
            _const_spec((FFN_HIDDEN, D_MODEL)),
        ],
        out_specs=pl.BlockSpec((None, SUBTILES * TILE, D_MODEL), lambda b, i: (b, i, 0)),
        out_shape=jax.ShapeDtypeStruct((B, nt * TILE, D_MODEL), F32),
        compiler_params=_cparams(("parallel", "parallel")),
        name="outproj_ffn",
    )(*tokens, x_ctx, g2, g_final, w_out, w_gate, w_up, w_down)


def _rope_tables(seq):
    rows = seq // GRID_W
    row = jnp.repeat(jnp.arange(rows), GRID_W).astype(F32)
    col = jnp.tile(jnp.arange(GRID_W), rows).astype(F32)
    n = ROPE_AXIS_DIM // 2
    inv = ROPE_THETA ** (-jnp.arange(n, dtype=F32) / n)
    ang_r = row[:, None] * inv[None, :]
    ang_c = col[:, None] * inv[None, :]
    cr, sr, cc, sc = jnp.cos(ang_r), jnp.sin(ang_r), jnp.cos(ang_c), jnp.sin(ang_c)
    cos = jnp.tile(jnp.concatenate([cr, cr, cc, cc], axis=1), (1, 2))
    sin = jnp.tile(jnp.concatenate([-sr, sr, -sc, sc], axis=1), (1, 2))
    cos = jnp.concatenate([jnp.ones((CTX_LEN, HEAD_DIM), F32), cos], axis=0)
    sin = jnp.concatenate([jnp.zeros((CTX_LEN, HEAD_DIM), F32), sin], axis=0)
    return cos, sin


def _block_diag(w):
    eye = jnp.eye(w.shape[0], dtype=w.dtype)
    return jnp.einsum('hij,hk->hikj', w, eye).reshape(w.shape[0] * w.shape[1],
                                                      w.shape[0] * w.shape[2])


def kernel(x, c, ctx, c_ctx, w_ada, b_ada, g_norm1, g_norm2, w_in, lam_q, lam_k, g_attn, conv_w,
           conv_b, w_rg_a, b_rg_a, w_rg_x, b_rg_x, lru_lambda, g_sgu, w_spatial, b_spatial, w_out,
           w_ffn_gate, w_ffn_up, w_ffn_down, g_final):
    B, S, D = x.shape
    assert (D, ctx.shape[1], S % KEY_CHUNK) == (D_MODEL, CTX_LEN, 0) and B <= 8
    cos_t, sin_t = _rope_tables(S)
    cs = jnp.zeros((16, D), F32).at[:B].set(c).at[8].set(c_ctx)
    mods = _ada(cs, w_ada, b_ada)
    ones_bd = _block_diag(jnp.ones((SGU_GROUPS, SGU_GROUP_DIM, SGU_GROUP_DIM), BF16))
    stream = (ctx, x, 1)

    for l in range(DEPTH):
        last = l == DEPTH - 1
        lam_init = 0.8 - 0.6 * math.exp(-0.3 * l)
        mod = mods[l].reshape(16, N_MOD, 1, D)
        w_sp = jnp.transpose(w_spatial[l], (1, 0, 2)).reshape(CHUNK, SGU_GROUPS * CHUNK)
        b_sp = jnp.repeat(b_spatial[l].T, SGU_GROUP_DIM, axis=1)
        w_gate = jnp.stack([jnp.concatenate([_block_diag(w_rg_a[l, d]), _block_diag(w_rg_x[l, d])],
                                            axis=1) for d in range(2)]).astype(BF16)
        b_gate = jnp.concatenate([b_rg_a[l], b_rg_x[l]], axis=1)[:, None, :]

        qt, k, vt, lru_in, ys = _inproj(
            *stream, mod, g_norm1[l][None], w_in[l].astype(BF16), cos_t, sin_t, g_sgu[l][None],
            ones_bd, w_sp.astype(BF16), b_sp)
        t_off = 1 if last else 0
        ya = _attention(qt, k, vt, lam_q[l], lam_k[l], g_attn[l][None], lam_init, t_off)
        yr = _lru(lru_in, conv_w[l], conv_b[l][None], w_gate, b_gate, lru_lambda[l][:, None, :])
        xs = _ffn(ya, yr, ys, *stream, mod, g_norm2[l][None], g_final[None],
                  w_out[l].astype(BF16), w_ffn_gate[l].astype(BF16), w_ffn_up[l].astype(BF16),
                  w_ffn_down[l].astype(BF16), t_off, last)
        stream = (xs, xs, 0)
    return xs
```

```python
import functools
import math

import jax
import jax.numpy as jnp
from jax import lax
from jax.experimental import pallas as pl
from jax.experimental.pallas import tpu as pltpu

F32 = jnp.float32
BF16 = jnp.bfloat16

D_MODEL = 1024
DEPTH = 2
GRID_W = 64
CTX_LEN = 256
N_MOD = 6
EPS = 1e-6
ATTN_WIDTH = 512
LRU_WIDTH = 256
SGU_WIDTH = 256
ATTN_HEADS = 4
ATTN_HEADS_PER_STEP = 4
HEAD_DIM = 128
QK_DIM = 64
ROPE_AXIS_DIM = 32
ROPE_THETA = 10000.0
LRU_HEADS = 4
LRU_BLOCK = 64
CONV_WIDTH = 4
LRU_C = 8.0
SGU_GROUPS = 4
SGU_GROUP_DIM = 64
CHUNK = 128
FFN_HIDDEN = 2816
PROJ_WIDTH = 2560

TILE = 256
SUBTILES = 2
KEY_CHUNK = 2 * TILE
SUBLANES = 8
VMEM_LIMIT = 56 * 1024 * 1024


def _cparams(sem, flags=None):
    return pltpu.CompilerParams(dimension_semantics=sem, vmem_limit_bytes=VMEM_LIMIT, flags=flags)


def _const_spec(shape):
    nd = len(shape)
    return pl.BlockSpec(shape, lambda *_: (0,) * nd, pipeline_mode=pl.Buffered(1))


def _layer_spec(layer, shape):
    nd = len(shape)
    return pl.BlockSpec((None, *shape), lambda *_: (layer,) + (0,) * nd,
                        pipeline_mode=pl.Buffered(1))


def _sub_tile(i, j, t_off, n_tiles):
    return jnp.minimum(SUBTILES * i + j + t_off, n_tiles - 1)


def _token_spec(tile, width, first=0):
    return pl.BlockSpec((None, TILE, width), lambda b, i: (b, tile(i) - first, 0))


def _latent_spec(tile, lat_first):
    return pl.BlockSpec((None, TILE, D_MODEL),
                        lambda b, i: (b, jnp.maximum(tile(i), 1) - lat_first, 0))


def _mod_spec(tile):
    return pl.BlockSpec((None, N_MOD, 1, D_MODEL),
                        lambda b, i: (jnp.where(tile(i) == 0, 8, b), 0, 0, 0))


_CTX_SPEC = pl.BlockSpec((None, TILE, D_MODEL), lambda b, i: (b, 0, 0))


def _run_subtiles(body, t_off, n_tiles):
    rem = (n_tiles - t_off) % SUBTILES
    if rem == 0:
        body(SUBTILES)
    else:
        full = SUBTILES * (pl.program_id(1) + 1) + t_off <= n_tiles
        pl.when(full)(lambda: body(SUBTILES))
        pl.when(jnp.logical_not(full))(lambda: body(rem))


def _ada_kernel(c_ref, w_ref, b_ref, o_ref):
    c = c_ref[...]
    h = (c * jax.nn.sigmoid(c)).astype(BF16)
    o_ref[...] = jnp.dot(h, w_ref[...].astype(BF16), preferred_element_type=F32) + b_ref[...]


def _ada(cs, w_ada, b_ada):
    tn = 1536
    n_out = N_MOD * D_MODEL
    return pl.pallas_call(
        _ada_kernel,
        grid=(DEPTH, n_out // tn),
        in_specs=[
            pl.BlockSpec((16, D_MODEL), lambda l, n: (0, 0)),
            pl.BlockSpec((None, D_MODEL, tn), lambda l, n: (l, 0, n)),
            pl.BlockSpec((None, 1, tn), lambda l, n: (l, 0, n)),
        ],
        out_specs=pl.BlockSpec((None, 16, tn), lambda l, n: (l, 0, n)),
        out_shape=jax.ShapeDtypeStruct((DEPTH, 16, n_out), F32),
        compiler_params=_cparams(("arbitrary", "arbitrary")),
        name="ada_mod",
    )(cs, w_ada, b_ada.reshape(DEPTH, 1, n_out))


def _rope(x, cos, sin, first_half):
    partner = jnp.where(first_half, pltpu.roll(x, 128 - 16, 1), pltpu.roll(x, 16, 1))
    return x * cos + partner * sin


def _inproj_kernel(*refs, n_tiles):
    _run_subtiles(lambda n_sub: _inproj_tiles(refs, n_sub), 0, n_tiles)


def _inproj_tiles(refs, n_sub):
    per_sub = 4
    subs = [refs[per_sub * j:per_sub * (j + 1)] for j in range(n_sub)]
    (xc_ref, g1_ref, w_ref, gs_ref, ones_ref, wsp_ref, bsp_ref,
     qt_ref, k_ref, vt_ref, lru_ref, sgu_ref) = refs[per_sub * SUBTILES:]
    first_tile = SUBTILES * pl.program_id(1)

    p_all = []
    for j, (xl_ref, mod_ref, _, _) in enumerate(subs):
        x = jnp.where(first_tile + j == 0, xc_ref[...], xl_ref[...])
        rs = lax.rsqrt(jnp.mean(x * x, axis=-1, keepdims=True) + EPS)
        h = (x * rs * g1_ref[...]) * (1.0 + mod_ref[1]) + mod_ref[0]
        p_all.append(jnp.dot(h.astype(BF16), w_ref[...], preferred_element_type=F32))

    lane = lax.broadcasted_iota(jnp.int32, (TILE, HEAD_DIM), 1)
    first_half = (lane & 16) == 0
    grp = lax.broadcasted_iota(jnp.int32, (CHUNK, SGU_WIDTH), 1) // SGU_GROUP_DIM
    scale = QK_DIM ** -0.5 * math.log2(math.e)
    for j, (_, _, cos_ref, sin_ref) in enumerate(subs):
        p = p_all[j]
        rows = slice(j * TILE, (j + 1) * TILE)
        cos = cos_ref[...]
        sin = sin_ref[...]
        q_parts, k_parts = [], []
        for hd in range(ATTN_HEADS):
            q = p[:, hd * HEAD_DIM:(hd + 1) * HEAD_DIM]
            k = p[:, ATTN_WIDTH + hd * HEAD_DIM:ATTN_WIDTH + (hd + 1) * HEAD_DIM]
            q_parts.append(_rope(q, cos, sin, first_half) * scale)
            k_parts.append(_rope(k, cos, sin, first_half))
        q = jnp.concatenate(q_parts, axis=1)
        qt_ref[j] = q.T.astype(BF16)
        k_ref[rows, :] = jnp.concatenate(k_parts, axis=1).astype(BF16)
        vt_ref[j] = p[:, 2 * ATTN_WIDTH:3 * ATTN_WIDTH].T.astype(BF16)

        lx0 = 3 * ATTN_WIDTH
        lru_ref[rows, 0:LRU_WIDTH] = p[:, lx0:lx0 + LRU_WIDTH]
        lru_ref[rows, LRU_WIDTH:] = jax.nn.gelu(p[:, lx0 + LRU_WIDTH:lx0 + 2 * LRU_WIDTH])

        su0 = lx0 + 2 * LRU_WIDTH
        u = jax.nn.gelu(p[:, su0:su0 + SGU_WIDTH])
        v = jax.nn.gelu(p[:, su0 + SGU_WIDTH:su0 + 2 * SGU_WIDTH])
        ss = v * v
        hi = ss.astype(BF16)
        lo = (ss - hi.astype(F32)).astype(BF16)
        gsum = (jnp.dot(hi, ones_ref[...], preferred_element_type=F32)
                + jnp.dot(lo, ones_ref[...], preferred_element_type=F32))
        vn = v * lax.rsqrt(gsum * (1.0 / SGU_GROUP_DIM) + EPS) * gs_ref[...]
        for c in range(TILE // CHUNK):
            vc = vn[c * CHUNK:(c + 1) * CHUNK]
            rhs = jnp.concatenate(
                [jnp.where(grp == g, vc, 0.0) for g in range(SGU_GROUPS)], axis=0).astype(BF16)
            m = jnp.dot(wsp_ref[...], rhs, preferred_element_type=F32) + bsp_ref[...]
            r0 = j * TILE + c * CHUNK
            sgu_ref[r0:r0 + CHUNK, :] = (u[c * CHUNK:(c + 1) * CHUNK] * m).astype(BF16)


def _inproj(x_ctx, x_lat, lat_first, mod, g1, layer, w_in, cos_t, sin_t, g_sgu, ones_bd, w_sp,
            b_sp):
    B = x_lat.shape[0]
    T = cos_t.shape[0]
    nt = T // TILE

    def sub_specs(j):
        tile = lambda i: _sub_tile(i, j, 0, nt)
        table = pl.BlockSpec((TILE, HEAD_DIM), lambda b, i: (tile(i), 0))
        return [_latent_spec(tile, lat_first), _mod_spec(tile), table, table]

    tokens = []
    for j in range(SUBTILES):
        tokens += [x_lat, mod, cos_t, sin_t]
    rows = SUBTILES * TILE
    return pl.pallas_call(
        functools.partial(_inproj_kernel, n_tiles=nt),
        grid=(B, pl.cdiv(nt, SUBTILES)),
        in_specs=[
            *[spec for j in range(SUBTILES) for spec in sub_specs(j)],
            _CTX_SPEC,
            _const_spec((1, D_MODEL)),
            _layer_spec(layer, (D_MODEL, PROJ_WIDTH)),
            _const_spec((1, SGU_WIDTH)),
            _const_spec((SGU_WIDTH, SGU_WIDTH)),
            _const_spec((CHUNK, SGU_GROUPS * CHUNK)),
            _const_spec((CHUNK, SGU_WIDTH)),
        ],
        out_specs=[
            pl.BlockSpec((None, SUBTILES, ATTN_WIDTH, TILE), lambda b, i: (b, i, 0, 0)),
            pl.BlockSpec((None, rows, ATTN_WIDTH), lambda b, i: (b, i, 0)),
            pl.BlockSpec((None, SUBTILES, ATTN_WIDTH, TILE), lambda b, i: (b, i, 0, 0)),
            pl.BlockSpec((None, rows, 2 * LRU_WIDTH), lambda b, i: (b, i, 0)),
            pl.BlockSpec((None, rows, SGU_WIDTH), lambda b, i: (b, i, 0)),
        ],
        out_shape=[
            jax.ShapeDtypeStruct((B, nt, ATTN_WIDTH, TILE), BF16),
            jax.ShapeDtypeStruct((B, T, ATTN_WIDTH), BF16),
            jax.ShapeDtypeStruct((B, nt, ATTN_WIDTH, TILE), BF16),
            jax.ShapeDtypeStruct((B, T, 2 * LRU_WIDTH), F32),
            jax.ShapeDtypeStruct((B, T, SGU_WIDTH), BF16),
        ],
        compiler_params=_cparams(("parallel", "parallel")),
        name="inproj",
    )(*tokens, x_ctx, g1, w_in, g_sgu, ones_bd, w_sp, b_sp)


MAX_SCORE_GAP = 60.0

BF16_ROWS = 16
ATTN_Q_TILES = 1
QW = ATTN_Q_TILES * TILE
_ATTN_HEAD_SCRATCH = [
    pltpu.VMEM((HEAD_DIM + BF16_ROWS, 2 * QW), F32),
    pltpu.VMEM((1, 2 * QW), F32),
    pltpu.VMEM((SUBLANES, 2 * QW), F32),
    pltpu.VMEM((HEAD_DIM, 2 * QW), BF16)]


def _attn_kernel(*refs, lam_init, ctx_step):
    q_refs = refs[:ATTN_Q_TILES]
    k_ref, v_ref, lq_ref, lk_ref, g_ref, o_ref = refs[ATTN_Q_TILES:ATTN_Q_TILES + 6]
    scratch = refs[ATTN_Q_TILES + 6:]
    heads = range(ATTN_HEADS_PER_STEP)
    per_head = len(_ATTN_HEAD_SCRATCH)
    acc_ref, m_ref, smax_ref, rhs_ref = [scratch[n::per_head] for n in range(per_head)]
    hsl = [slice(h * HEAD_DIM, (h + 1) * HEAD_DIM) for h in heads]
    row = lax.broadcasted_iota(jnp.int32, (HEAD_DIM, QW), 0)
    for h in heads:
        qf = jnp.concatenate([q[hsl[h], :] for q in q_refs], axis=1).astype(F32)
        rhs_ref[h][...] = jnp.concatenate([jnp.where(row < QK_DIM, qf, 0.0),
                                           jnp.where(row >= QK_DIM, qf, 0.0)],
                                          axis=1).astype(BF16)

    n_latent = (k_ref.shape[0] - CTX_LEN) // KEY_CHUNK

    def scores(h, c):
        if isinstance(c, int) and c == 0:
            rows = slice(0, CTX_LEN)
        else:
            rows = pl.ds(pl.multiple_of(CTX_LEN + (c - 1) * KEY_CHUNK, TILE), KEY_CHUNK)
        return jnp.dot(k_ref[rows, hsl[h]], rhs_ref[h][...], preferred_element_type=F32)

    def exponentiate(s, m):
        return jnp.exp2(s - m).astype(BF16)

    def accumulate(h, c, p):
        ctx = isinstance(c, int) and c == 0
        v = (v_ref[0, hsl[h], :] if ctx else
             jnp.concatenate([v_ref[2 * c - 1, hsl[h], :], v_ref[2 * c, hsl[h], :]], axis=1))
        v = jnp.concatenate([v, jnp.ones((BF16_ROWS, v.shape[1]), BF16)], axis=0)
        upd = jnp.dot(v, p, preferred_element_type=F32)
        acc_ref[h][...] = upd if ctx else acc_ref[h][...] + upd

    def probabilities(h, c):
        s = scores(h, c)
        if c == 0:
            m_ref[h][...] = jnp.max(s, axis=0, keepdims=True)
        else:
            smax = jnp.max(s.reshape(KEY_CHUNK // SUBLANES, SUBLANES, 2 * QW), axis=0)
            smax_ref[h][...] = smax if c == 1 else jnp.maximum(smax_ref[h][...], smax)
        return exponentiate(s, m_ref[h][...])

    def stream(chunks):
        items = [(h, c) for c in chunks for h in heads]
        skew = 2
        ps = [probabilities(*item) for item in items[:skew]]
        for i, (h, c) in enumerate(items):
            if i + skew < len(items):
                ps.append(probabilities(*items[i + skew]))
            accumulate(h, c, ps[i])
            ps[i] = None

    def exact_softmax():
        for h in heads:
            m = jnp.maximum(m_ref[h][...], jnp.max(smax_ref[h][...], axis=0, keepdims=True))
            accumulate(h, 0, exponentiate(scores(h, 0), m))

            def chunk(c, carry, h=h, m=m):
                accumulate(h, c, exponentiate(scores(h, c), m))
                return carry

            lax.fori_loop(1, n_latent + 1, chunk, 0)

    def finish():
        t = jnp.sum(lq_ref[...] * lk_ref[...], axis=1, keepdims=True)
        e = jnp.exp(t)
        lam = e[0:1] - e[1:2] + lam_init
        for h in heads:
            acc = acc_ref[h][0:HEAD_DIM, :]
            inv = 1.0 / acc_ref[h][HEAD_DIM:HEAD_DIM + 1, :]
            ot = acc[:, :QW] * inv[:, :QW] - lam * (acc[:, QW:] * inv[:, QW:])
            rs = lax.rsqrt(jnp.mean(ot * ot, axis=0, keepdims=True) + EPS)
            y = (ot * rs).T * g_ref[...]
            o_ref[:, hsl[h]] = (y * (1.0 - lam_init)).astype(BF16)

    def context_queries():
        stream([0])
        finish()

    def latent_queries():
        stream(range(n_latent + 1))
        finish()
        gap = functools.reduce(
            jnp.maximum, [jnp.max(smax_ref[h][...] - m_ref[h][...]) for h in heads])

        @pl.when(gap > MAX_SCORE_GAP)
        def _():
            exact_softmax()
            finish()

    if ctx_step:
        pl.when(pl.program_id(2) == 0)(context_queries)
        pl.when(pl.program_id(2) != 0)(latent_queries)
    else:
        latent_queries()


def _attention(qt, k, vt, lam_q, lam_k, g_attn, lam_init, with_ctx):
    B, nt, _, _ = qt.shape
    T = k.shape[1]
    hp = ATTN_HEADS_PER_STEP
    ctx_step = 1 if with_ctx else 0
    steps = ctx_step + (nt - 1) // ATTN_Q_TILES
    assert (nt - 1) % ATTN_Q_TILES == 0
    kern = functools.partial(_attn_kernel, lam_init=lam_init, ctx_step=ctx_step)

    def q_spec(j):
        tile = lambda i: jnp.where(i < ctx_step, j, 1 + ATTN_Q_TILES * (i - ctx_step) + j)
        return pl.BlockSpec((None, None, hp * HEAD_DIM, TILE),
                            lambda b, h, i: (b, tile(i), h, 0))

    return pl.pallas_call(
        kern,
        grid=(B, ATTN_HEADS // hp, steps),
        in_specs=[
            *[q_spec(j) for j in range(ATTN_Q_TILES)],
            pl.BlockSpec((None, T, hp * HEAD_DIM), lambda b, h, i: (b, 0, h)),
            pl.BlockSpec((None, nt, hp * HEAD_DIM, TILE), lambda b, h, i: (b, 0, h, 0)),
            pl.BlockSpec((2, QK_DIM), lambda b, h, i: (0, 0)),
            pl.BlockSpec((2, QK_DIM), lambda b, h, i: (0, 0)),
            pl.BlockSpec((1, HEAD_DIM), lambda b, h, i: (0, 0)),
        ],
        out_specs=pl.BlockSpec((None, QW, hp * HEAD_DIM), lambda b, h, i: (b, i, h)),
        out_shape=jax.ShapeDtypeStruct((B, steps * QW, ATTN_WIDTH), BF16),
        scratch_shapes=[shape for _ in range(hp) for shape in _ATTN_HEAD_SCRATCH],
        compiler_params=_cparams(("parallel", "parallel", "arbitrary")),
        name="diff_attn",
    )(*[qt] * ATTN_Q_TILES, k, vt, lam_q, lam_k, g_attn)


def _expm1(x, u):
    near = jnp.where(u == 1.0, x, (u - 1.0) * x / jnp.log(u))
    return jnp.where(jnp.abs(x) > 0.5, u - 1.0, near)


def _group_roll(x, shift):
    n, w = x.shape
    return pltpu.roll(x.reshape(n // SUBLANES, SUBLANES, w), shift % SUBLANES, 1).reshape(n, w)


def _chunk_scan(a, bt, carry, sub, reverse):
    n = a.shape[0]
    acc_a, acc_b = a, bt
    for k in (1, 2, 4):
        if reverse:
            valid = sub <= SUBLANES - 1 - k
            shift = n - k
        else:
            valid = sub >= k
            shift = k
        a_sh = jnp.where(valid, _group_roll(acc_a, shift), 1.0)
        b_sh = jnp.where(valid, _group_roll(acc_b, shift), 0.0)
        acc_b = acc_a * b_sh + acc_b
        acc_a = acc_a * a_sh
    groups = n // SUBLANES
    order = range(groups - 1, -1, -1) if reverse else range(groups)
    edge = 0 if reverse else SUBLANES - 1
    carries = [None] * groups
    c = carry
    for g in order:
        carries[g] = jnp.broadcast_to(c, (SUBLANES, a.shape[1]))
        r = g * SUBLANES + edge
        c = acc_a[r:r + 1] * c + acc_b[r:r + 1]
    h = acc_a * jnp.concatenate(carries, axis=0) + acc_b
    return h, c


def _lru_kernel(x_ref, cw_ref, cb_ref, wg_ref, bg_ref, lam_ref, o_ref, hs_ref, conv_ref):
    T = x_ref.shape[0]
    nt = T // TILE
    row = lax.broadcasted_iota(jnp.int32, (TILE, LRU_WIDTH), 0)
    sub = row & (SUBLANES - 1)
    cw = cw_ref[...]
    cb = cb_ref[...]

    def conv(j):
        r0 = pl.multiple_of(j * TILE, TILE)
        cur = x_ref[pl.ds(r0, TILE), 0:LRU_WIDTH]
        pr = pl.multiple_of(jnp.maximum(r0 - SUBLANES, 0), SUBLANES)
        nx = pl.multiple_of(jnp.minimum(r0 + TILE, T - SUBLANES), SUBLANES)
        prev8 = x_ref[pl.ds(pr, SUBLANES), 0:LRU_WIDTH]
        next8 = x_ref[pl.ds(nx, SUBLANES), 0:LRU_WIDTH]
        has_prev = j >= 2
        has_next = jnp.logical_and(j >= 1, j <= nt - 2)
        p1 = jnp.where(has_prev, prev8[SUBLANES - 1:SUBLANES], 0.0)
        n0 = jnp.where(has_next, next8[0:1], 0.0)
        n1 = jnp.where(has_next, next8[1:2], 0.0)
        xm1 = jnp.where(row == 0, p1, pltpu.roll(cur, 1, 0))
        xp1 = jnp.where(row == TILE - 1, n0, pltpu.roll(cur, TILE - 1, 0))
        xp2 = jnp.where(row == TILE - 2, n0,
                        jnp.where(row == TILE - 1, n1, pltpu.roll(cur, TILE - 2, 0)))
        return cw[0:1] * xm1 + cw[1:2] * cur + cw[2:3] * xp1 + cw[3:4] * xp2 + cb

    def direction(j, d, carry):
        rows = pl.ds(pl.multiple_of(j * TILE, TILE), TILE)
        if d == 0:
            c = conv(j)
            conv_ref[rows, :] = c
        else:
            c = conv_ref[rows, :]
        gates = jnp.dot(c.astype(BF16), wg_ref[d], preferred_element_type=F32) + bg_ref[d]
        r = jax.nn.sigmoid(gates[:, :LRU_WIDTH])
        i = jax.nn.sigmoid(gates[:, LRU_WIDTH:])
        log_a = (-LRU_C * r) * jax.nn.softplus(-lam_ref[d])
        a = jnp.exp(log_a)
        y = -_expm1(2.0 * log_a, a * a)
        root = jnp.where(y > 0.0, y * lax.rsqrt(y), 0.0)
        bt = root * (i * c)
        return _chunk_scan(a, bt, carry, sub, reverse=(d == 1))

    def fwd(j, carry):
        h, carry = direction(j, 0, carry)
        hs_ref[pl.ds(pl.multiple_of(j * TILE, TILE), TILE), :] = h
        return carry

    def rev(n, carry):
        j = jnp.where(n == 0, 0, nt - n)
        h, carry = direction(j, 1, carry)
        r0 = pl.multiple_of(j * TILE, TILE)
        gate = x_ref[pl.ds(r0, TILE), LRU_WIDTH:]
        o_ref[pl.ds(r0, TILE), :] = ((hs_ref[pl.ds(r0, TILE), :] + h) * gate).astype(BF16)
        return carry

    zero = jnp.zeros((1, LRU_WIDTH), F32)
    lax.fori_loop(0, nt, fwd, zero)
    lax.fori_loop(0, nt, rev, zero)


def _lru(lru_in, conv_w, conv_b, w_gate, b_gate, lam):
    B, T, _ = lru_in.shape
    return pl.pallas_call(
        _lru_kernel,
        grid=(B,),
        in_specs=[
            pl.BlockSpec((None, T, 2 * LRU_WIDTH), lambda b: (b, 0, 0)),
            _const_spec((CONV_WIDTH, LRU_WIDTH)),
            _const_spec((1, LRU_WIDTH)),
            _const_spec((2, LRU_WIDTH, 2 * LRU_WIDTH)),
            _const_spec((2, 1, 2 * LRU_WIDTH)),
            _const_spec((2, 1, LRU_WIDTH)),
        ],
        out_specs=pl.BlockSpec((None, T, LRU_WIDTH), lambda b: (b, 0, 0)),
        out_shape=jax.ShapeDtypeStruct((B, T, LRU_WIDTH), BF16),
        scratch_shapes=[pltpu.VMEM((T, LRU_WIDTH), F32),
                        pltpu.VMEM((T, LRU_WIDTH), F32)],
        compiler_params=_cparams(("parallel",)),
        name="rglru",
    )(lru_in, conv_w, conv_b, w_gate, b_gate, lam)


def _ffn_kernel(*refs, t_off, n_tiles, final):
    _run_subtiles(lambda n_sub: _ffn_tiles(refs, n_sub, t_off, final), t_off, n_tiles)


def _ffn_tiles(refs, n_sub, t_off, final):
    per_sub = 5
    subs = [refs[per_sub * j:per_sub * (j + 1)] for j in range(n_sub)]
    xc_ref, g2_ref, gf_ref, wo_ref, wg_ref, wu_ref, wd_ref, o_ref = refs[per_sub * SUBTILES:]
    first_tile = SUBTILES * pl.program_id(1) + t_off

    proj = [jnp.dot(jnp.concatenate([ya[...], yr[...], ys[...]], axis=-1), wo_ref[...],
                    preferred_element_type=F32) for ya, yr, ys, _, _ in subs]
    x1, g, u = [], [], []
    for j, (_, _, _, xl_ref, mod_ref) in enumerate(subs):
        x = jnp.where(first_tile + j == 0, xc_ref[...], xl_ref[...])
        x1.append(x + mod_ref[2] * proj[j])
        rs = lax.rsqrt(jnp.mean(x1[j] * x1[j], axis=-1, keepdims=True) + EPS)
        h = ((x1[j] * rs * g2_ref[...]) * (1.0 + mod_ref[4]) + mod_ref[3]).astype(BF16)
        g.append(jnp.dot(h, wg_ref[...], preferred_element_type=F32))
        u.append(jnp.dot(h, wu_ref[...], preferred_element_type=F32))
    for j, (_, _, _, _, mod_ref) in enumerate(subs):
        act = ((g[j] * jax.nn.sigmoid(g[j])) * u[j]).astype(BF16)
        x2 = x1[j] + mod_ref[5] * jnp.dot(act, wd_ref[...], preferred_element_type=F32)
        if final:
            x2 = x2 * lax.rsqrt(jnp.mean(x2 * x2, axis=-1, keepdims=True) + EPS) * gf_ref[...]
        o_ref[j * TILE:(j + 1) * TILE, :] = x2


def _ffn(ya, yr, ys, x_ctx, x_lat, lat_first, mod, g2, g_final, layer, w_out, w_gate, w_up,
         w_down, t_off, final):
    B, T, _ = yr.shape
    n_tiles = T // TILE
    nt = n_tiles - t_off
    with_ctx = t_off == 0
    ya_block = ((lambda t: jnp.where(t == 0, 0, ATTN_Q_TILES + t - 1)) if with_ctx
                else (lambda t: t - 1))
    kern = functools.partial(_ffn_kernel, t_off=t_off, n_tiles=n_tiles, final=final)

    def sub_specs(j):
        tile = lambda i: _sub_tile(i, j, t_off, n_tiles)
        return [pl.BlockSpec((None, TILE, ATTN_WIDTH), lambda b, i: (b, ya_block(tile(i)), 0)),
                _token_spec(tile, LRU_WIDTH),
                _token_spec(tile, SGU_WIDTH), _latent_spec(tile, lat_first), _mod_spec(tile)]

    tokens = []
    for j in range(SUBTILES):
        tokens += [ya, yr, ys, x_lat, mod]
    return pl.pallas_call(
        kern,
        grid=(B, pl.cdiv(nt, SUBTILES)),
        in_specs=[
            *[spec for j in range(SUBTILES) for spec in sub_specs(j)],
            _CTX_SPEC,
            _const_spec((1, D_MODEL)),
            _const_spec((1, D_MODEL)),
            _layer_spec(layer, (D_MODEL, D_MODEL)),
            _layer_spec(layer, (D_MODEL, FFN_HIDDEN)),
            _layer_spec(layer, (D_MODEL, FFN_HIDDEN)),
            _layer_spec(layer, (FFN_HIDDEN, D_MODEL)),
        ],
        out_specs=pl.BlockSpec((None, SUBTILES * TILE, D_MODEL), lambda b, i: (b, i, 0)),
        out_shape=jax.ShapeDtypeStruct((B, nt * TILE, D_MODEL), F32),
        compiler_params=_cparams(("parallel", "parallel")),
        name="outproj_ffn",
    )(*tokens, x_ctx, g2, g_final, w_out, w_gate, w_up, w_down)


def _rope_tables(seq):
    rows = seq // GRID_W
    row = jnp.repeat(jnp.arange(rows), GRID_W).astype(F32)
    col = jnp.tile(jnp.arange(GRID_W), rows).astype(F32)
    n = ROPE_AXIS_DIM // 2
    inv = ROPE_THETA ** (-jnp.arange(n, dtype=F32) / n)
    ang_r = row[:, None] * inv[None, :]
    ang_c = col[:, None] * inv[None, :]
    cr, sr, cc, sc = jnp.cos(ang_r), jnp.sin(ang_r), jnp.cos(ang_c), jnp.sin(ang_c)
    cos = jnp.tile(jnp.concatenate([cr, cr, cc, cc], axis=1), (1, 2))
    sin = jnp.tile(jnp.concatenate([-sr, sr, -sc, sc], axis=1), (1, 2))
    cos = jnp.concatenate([jnp.ones((CTX_LEN, HEAD_DIM), F32), cos], axis=0)
    sin = jnp.concatenate([jnp.zeros((CTX_LEN, HEAD_DIM), F32), sin], axis=0)
    return cos, sin


def _block_diag(w):
    eye = jnp.eye(w.shape[0], dtype=w.dtype)
    return jnp.einsum('hij,hk->hikj', w, eye).reshape(w.shape[0] * w.shape[1],
                                                      w.shape[0] * w.shape[2])


def kernel(x, c, ctx, c_ctx, w_ada, b_ada, g_norm1, g_norm2, w_in, lam_q, lam_k, g_attn, conv_w,
           conv_b, w_rg_a, b_rg_a, w_rg_x, b_rg_x, lru_lambda, g_sgu, w_spatial, b_spatial, w_out,
           w_ffn_gate, w_ffn_up, w_ffn_down, g_final):
    B, S, D = x.shape
    assert (D, ctx.shape[1], S % KEY_CHUNK) == (D_MODEL, CTX_LEN, 0) and B <= 8
    cos_t, sin_t = _rope_tables(S)
    cs = jnp.zeros((16, D), F32).at[:B].set(c).at[8].set(c_ctx)
    mods = _ada(cs, w_ada, b_ada)
    ones_bd = _block_diag(jnp.ones((SGU_GROUPS, SGU_GROUP_DIM, SGU_GROUP_DIM), BF16))
    w_in16, w_out16 = w_in.astype(BF16), w_out.astype(BF16)
    w_gate16, w_up16, w_down16 = (w.astype(BF16) for w in (w_ffn_gate, w_ffn_up, w_ffn_down))
    stream = (ctx, x, 1)

    for l in range(DEPTH):
        last = l == DEPTH - 1
        lam_init = 0.8 - 0.6 * math.exp(-0.3 * l)
        mod = mods[l].reshape(16, N_MOD, 1, D)
        w_sp = jnp.transpose(w_spatial[l], (1, 0, 2)).reshape(CHUNK, SGU_GROUPS * CHUNK)
        b_sp = jnp.repeat(b_spatial[l].T, SGU_GROUP_DIM, axis=1)
        w_gate = jnp.stack([jnp.concatenate([_block_diag(w_rg_a[l, d]), _block_diag(w_rg_x[l, d])],
                                            axis=1) for d in range(2)]).astype(BF16)
        b_gate = jnp.concatenate([b_rg_a[l], b_rg_x[l]], axis=1)[:, None, :]

        qt, k, vt, lru_in, ys = _inproj(
            *stream, mod, g_norm1[l][None], l, w_in16, cos_t, sin_t, g_sgu[l][None],
            ones_bd, w_sp.astype(BF16), b_sp)
        t_off = 1 if last else 0
        ya = _attention(qt, k, vt, lam_q[l], lam_k[l], g_attn[l][None], lam_init, not last)
        yr = _lru(lru_in, conv_w[l], conv_b[l][None], w_gate, b_gate, lru_lambda[l][:, None, :])
        xs = _ffn(ya, yr, ys, *stream, mod, g_norm2[l][None], g_final[None], l, w_out16,
                  w_gate16, w_up16, w_down16, t_off, last)
        stream = (xs, xs, 0)
    return xs
```

```python
import functools
import math

import jax
import jax.numpy as jnp
from jax import lax
from jax.experimental import pallas as pl
from jax.experimental.pallas import tpu as pltpu

F32 = jnp.float32
BF16 = jnp.bfloat16

D_MODEL = 1024
DEPTH = 2
GRID_W = 64
CTX_LEN = 256
N_MOD = 6
EPS = 1e-6
ATTN_WIDTH = 512
LRU_WIDTH = 256
SGU_WIDTH = 256
ATTN_HEADS = 4
ATTN_HEADS_PER_STEP = 4
HEAD_DIM = 128
QK_DIM = 64
ROPE_AXIS_DIM = 32
ROPE_THETA = 10000.0
LRU_HEADS = 4
LRU_BLOCK = 64
CONV_WIDTH = 4
LRU_C = 8.0
SGU_GROUPS = 4
SGU_GROUP_DIM = 64
CHUNK = 128
FFN_HIDDEN = 2816
PROJ_WIDTH = 2560

TILE = 256
SUBTILES = 2
KEY_CHUNK = 2 * TILE
SUBLANES = 8
VMEM_LIMIT = 56 * 1024 * 1024


def _cparams(sem, flags=None):
    return pltpu.CompilerParams(dimension_semantics=sem, vmem_limit_bytes=VMEM_LIMIT, flags=flags)


def _const_spec(shape):
    nd = len(shape)
    return pl.BlockSpec(shape, lambda *_: (0,) * nd, pipeline_mode=pl.Buffered(1))


def _layer_spec(layer, shape):
    nd = len(shape)
    return pl.BlockSpec((None, *shape), lambda *_: (layer,) + (0,) * nd,
                        pipeline_mode=pl.Buffered(1))


def _sub_tile(i, j, t_off, n_tiles):
    return jnp.minimum(SUBTILES * i + j + t_off, n_tiles - 1)


def _token_spec(tile, width, first=0):
    return pl.BlockSpec((None, TILE, width), lambda b, i: (b, tile(i) - first, 0))


def _latent_spec(tile, lat_first):
    return pl.BlockSpec((None, TILE, D_MODEL),
                        lambda b, i: (b, jnp.maximum(tile(i), 1) - lat_first, 0))


def _mod_spec(tile):
    return pl.BlockSpec((None, N_MOD, 1, D_MODEL),
                        lambda b, i: (jnp.where(tile(i) == 0, 8, b), 0, 0, 0))


_CTX_SPEC = pl.BlockSpec((None, TILE, D_MODEL), lambda b, i: (b, 0, 0))


def _run_subtiles(body, t_off, n_tiles):
    rem = (n_tiles - t_off) % SUBTILES
    if rem == 0:
        body(SUBTILES)
    else:
        full = SUBTILES * (pl.program_id(1) + 1) + t_off <= n_tiles
        pl.when(full)(lambda: body(SUBTILES))
        pl.when(jnp.logical_not(full))(lambda: body(rem))


def _ada_kernel(c_ref, w_ref, b_ref, o_ref):
    c = c_ref[...]
    h = (c * jax.nn.sigmoid(c)).astype(BF16)
    o_ref[...] = jnp.dot(h, w_ref[...].astype(BF16), preferred_element_type=F32) + b_ref[...]


def _ada(cs, w_ada, b_ada):
    tn = 1536
    n_out = N_MOD * D_MODEL
    return pl.pallas_call(
        _ada_kernel,
        grid=(DEPTH, n_out // tn),
        in_specs=[
            pl.BlockSpec((16, D_MODEL), lambda l, n: (0, 0)),
            pl.BlockSpec((None, D_MODEL, tn), lambda l, n: (l, 0, n)),
            pl.BlockSpec((None, 1, tn), lambda l, n: (l, 0, n)),
        ],
        out_specs=pl.BlockSpec((None, 16, tn), lambda l, n: (l, 0, n)),
        out_shape=jax.ShapeDtypeStruct((DEPTH, 16, n_out), F32),
        compiler_params=_cparams(("arbitrary", "arbitrary")),
        name="ada_mod",
    )(cs, w_ada, b_ada.reshape(DEPTH, 1, n_out))


def _rope(x, cos, sin, first_half):
    partner = jnp.where(first_half, pltpu.roll(x, 128 - 16, 1), pltpu.roll(x, 16, 1))
    return x * cos + partner * sin


def _inproj_kernel(*refs, n_tiles):
    _run_subtiles(lambda n_sub: _inproj_tiles(refs, n_sub), 0, n_tiles)


def _inproj_tiles(refs, n_sub):
    per_sub = 4
    subs = [refs[per_sub * j:per_sub * (j + 1)] for j in range(n_sub)]
    (xc_ref, g1_ref, w_ref, gs_ref, ones_ref, wsp_ref, bsp_ref,
     qt_ref, k_ref, vt_ref, lru_ref, sgu_ref) = refs[per_sub * SUBTILES:]
    first_tile = SUBTILES * pl.program_id(1)

    p_all = []
    for j, (xl_ref, mod_ref, _, _) in enumerate(subs):
        x = jnp.where(first_tile + j == 0, xc_ref[...], xl_ref[...])
        rs = lax.rsqrt(jnp.mean(x * x, axis=-1, keepdims=True) + EPS)
        h = (x * rs * g1_ref[...]) * (1.0 + mod_ref[1]) + mod_ref[0]
        p_all.append(jnp.dot(h.astype(BF16), w_ref[...], preferred_element_type=F32))

    lane = lax.broadcasted_iota(jnp.int32, (TILE, HEAD_DIM), 1)
    first_half = (lane & 16) == 0
    grp = lax.broadcasted_iota(jnp.int32, (CHUNK, SGU_WIDTH), 1) // SGU_GROUP_DIM
    scale = QK_DIM ** -0.5 * math.log2(math.e)
    for j, (_, _, cos_ref, sin_ref) in enumerate(subs):
        p = p_all[j]
        rows = slice(j * TILE, (j + 1) * TILE)
        cos = cos_ref[...]
        sin = sin_ref[...]
        q_parts, k_parts = [], []
        for hd in range(ATTN_HEADS):
            q = p[:, hd * HEAD_DIM:(hd + 1) * HEAD_DIM]
            k = p[:, ATTN_WIDTH + hd * HEAD_DIM:ATTN_WIDTH + (hd + 1) * HEAD_DIM]
            q_parts.append(_rope(q, cos, sin, first_half) * scale)
            k_parts.append(_rope(k, cos, sin, first_half))
        q = jnp.concatenate(q_parts, axis=1)
        qt_ref[j] = q.T.astype(BF16)
        k_ref[rows, :] = jnp.concatenate(k_parts, axis=1).astype(BF16)
        vt_ref[j] = p[:, 2 * ATTN_WIDTH:3 * ATTN_WIDTH].T.astype(BF16)

        lx0 = 3 * ATTN_WIDTH
        lru_ref[rows, 0:LRU_WIDTH] = p[:, lx0:lx0 + LRU_WIDTH]
        lru_ref[rows, LRU_WIDTH:] = jax.nn.gelu(p[:, lx0 + LRU_WIDTH:lx0 + 2 * LRU_WIDTH])

        su0 = lx0 + 2 * LRU_WIDTH
        u = jax.nn.gelu(p[:, su0:su0 + SGU_WIDTH])
        v = jax.nn.gelu(p[:, su0 + SGU_WIDTH:su0 + 2 * SGU_WIDTH])
        ss = v * v
        hi = ss.astype(BF16)
        lo = (ss - hi.astype(F32)).astype(BF16)
        gsum = (jnp.dot(hi, ones_ref[...], preferred_element_type=F32)
                + jnp.dot(lo, ones_ref[...], preferred_element_type=F32))
        vn = v * lax.rsqrt(gsum * (1.0 / SGU_GROUP_DIM) + EPS) * gs_ref[...]
        for c in range(TILE // CHUNK):
            vc = vn[c * CHUNK:(c + 1) * CHUNK]
            rhs = jnp.concatenate(
                [jnp.where(grp == g, vc, 0.0) for g in range(SGU_GROUPS)], axis=0).astype(BF16)
            m = jnp.dot(wsp_ref[...], rhs, preferred_element_type=F32) + bsp_ref[...]
            r0 = j * TILE + c * CHUNK
            sgu_ref[r0:r0 + CHUNK, :] = (u[c * CHUNK:(c + 1) * CHUNK] * m).astype(BF16)


def _inproj(x_ctx, x_lat, lat_first, mod, g1, layer, w_in, cos_t, sin_t, g_sgu, ones_bd, w_sp,
            b_sp):
    B = x_lat.shape[0]
    T = cos_t.shape[0]
    nt = T // TILE

    def sub_specs(j):
        tile = lambda i: _sub_tile(i, j, 0, nt)
        table = pl.BlockSpec((TILE, HEAD_DIM), lambda b, i: (tile(i), 0))
        return [_latent_spec(tile, lat_first), _mod_spec(tile), table, table]

    tokens = []
    for j in range(SUBTILES):
        tokens += [x_lat, mod, cos_t, sin_t]
    rows = SUBTILES * TILE
    return pl.pallas_call(
        functools.partial(_inproj_kernel, n_tiles=nt),
        grid=(B, pl.cdiv(nt, SUBTILES)),
        in_specs=[
            *[spec for j in range(SUBTILES) for spec in sub_specs(j)],
            _CTX_SPEC,
            _const_spec((1, D_MODEL)),
            _layer_spec(layer, (D_MODEL, PROJ_WIDTH)),
            _const_spec((1, SGU_WIDTH)),
            _const_spec((SGU_WIDTH, SGU_WIDTH)),
            _const_spec((CHUNK, SGU_GROUPS * CHUNK)),
            _const_spec((CHUNK, SGU_WIDTH)),
        ],
        out_specs=[
            pl.BlockSpec((None, SUBTILES, ATTN_WIDTH, TILE), lambda b, i: (b, i, 0, 0)),
            pl.BlockSpec((None, rows, ATTN_WIDTH), lambda b, i: (b, i, 0)),
            pl.BlockSpec((None, SUBTILES, ATTN_WIDTH, TILE), lambda b, i: (b, i, 0, 0)),
            pl.BlockSpec((None, rows, 2 * LRU_WIDTH), lambda b, i: (b, i, 0)),
            pl.BlockSpec((None, rows, SGU_WIDTH), lambda b, i: (b, i, 0)),
        ],
        out_shape=[
            jax.ShapeDtypeStruct((B, nt, ATTN_WIDTH, TILE), BF16),
            jax.ShapeDtypeStruct((B, T, ATTN_WIDTH), BF16),
            jax.ShapeDtypeStruct((B, nt, ATTN_WIDTH, TILE), BF16),
            jax.ShapeDtypeStruct((B, T, 2 * LRU_WIDTH), F32),
            jax.ShapeDtypeStruct((B, T, SGU_WIDTH), BF16),
        ],
        compiler_params=_cparams(("parallel", "parallel")),
        name="inproj",
    )(*tokens, x_ctx, g1, w_in, g_sgu, ones_bd, w_sp, b_sp)


MAX_SCORE_GAP = 60.0

BF16_ROWS = 16
ATTN_Q_TILES = 2
ATTN_STREAMS = ATTN_Q_TILES * ATTN_HEADS_PER_STEP
QW = TILE
_ATTN_HEAD_SCRATCH = [
    pltpu.VMEM((HEAD_DIM + BF16_ROWS, 2 * QW), F32),
    pltpu.VMEM((1, 2 * QW), F32),
    pltpu.VMEM((SUBLANES, 2 * QW), F32),
    pltpu.VMEM((HEAD_DIM, 2 * QW), BF16)]


def _attn_kernel(*refs, lam_init, ctx_step):
    q_refs = refs[:ATTN_Q_TILES]
    k_ref, v_ref, lq_ref, lk_ref, g_ref, o_ref = refs[ATTN_Q_TILES:ATTN_Q_TILES + 6]
    scratch = refs[ATTN_Q_TILES + 6:]
    heads = range(ATTN_STREAMS)
    per_head = len(_ATTN_HEAD_SCRATCH)
    acc_ref, m_ref, smax_ref, rhs_ref = [scratch[n::per_head] for n in range(per_head)]
    hsl = [slice((h % ATTN_HEADS_PER_STEP) * HEAD_DIM, (h % ATTN_HEADS_PER_STEP + 1) * HEAD_DIM)
           for h in heads]
    qsl = [slice((h // ATTN_HEADS_PER_STEP) * TILE, (h // ATTN_HEADS_PER_STEP + 1) * TILE)
           for h in heads]
    row = lax.broadcasted_iota(jnp.int32, (HEAD_DIM, QW), 0)
    for h in heads:
        qf = q_refs[h // ATTN_HEADS_PER_STEP][hsl[h], :].astype(F32)
        rhs_ref[h][...] = jnp.concatenate([jnp.where(row < QK_DIM, qf, 0.0),
                                           jnp.where(row >= QK_DIM, qf, 0.0)],
                                          axis=1).astype(BF16)

    n_latent = (k_ref.shape[0] - CTX_LEN) // KEY_CHUNK

    def scores(h, c):
        if isinstance(c, int) and c == 0:
            rows = slice(0, CTX_LEN)
        else:
            rows = pl.ds(pl.multiple_of(CTX_LEN + (c - 1) * KEY_CHUNK, TILE), KEY_CHUNK)
        return jnp.dot(k_ref[rows, hsl[h]], rhs_ref[h][...], preferred_element_type=F32)

    def exponentiate(s, m):
        return jnp.exp2(s - m).astype(BF16)

    def accumulate(h, c, p):
        ctx = isinstance(c, int) and c == 0
        v = (v_ref[0, hsl[h], :] if ctx else
             jnp.concatenate([v_ref[2 * c - 1, hsl[h], :], v_ref[2 * c, hsl[h], :]], axis=1))
        v = jnp.concatenate([v, jnp.ones((BF16_ROWS, v.shape[1]), BF16)], axis=0)
        upd = jnp.dot(v, p, preferred_element_type=F32)
        acc_ref[h][...] = upd if ctx else acc_ref[h][...] + upd

    def probabilities(h, c):
        s = scores(h, c)
        if c == 0:
            m_ref[h][...] = jnp.max(s, axis=0, keepdims=True)
        else:
            smax = jnp.max(s.reshape(KEY_CHUNK // SUBLANES, SUBLANES, 2 * QW), axis=0)
            smax_ref[h][...] = smax if c == 1 else jnp.maximum(smax_ref[h][...], smax)
        return exponentiate(s, m_ref[h][...])

    def stream(chunks):
        items = [(h, c) for c in chunks for h in heads]
        skew = 2
        ps = [probabilities(*item) for item in items[:skew]]
        for i, (h, c) in enumerate(items):
            if i + skew < len(items):
                ps.append(probabilities(*items[i + skew]))
            accumulate(h, c, ps[i])
            ps[i] = None

    def exact_softmax():
        for h in heads:
            m = jnp.maximum(m_ref[h][...], jnp.max(smax_ref[h][...], axis=0, keepdims=True))
            accumulate(h, 0, exponentiate(scores(h, 0), m))

            def chunk(c, carry, h=h, m=m):
                accumulate(h, c, exponentiate(scores(h, c), m))
                return carry

            lax.fori_loop(1, n_latent + 1, chunk, 0)

    def finish():
        t = jnp.sum(lq_ref[...] * lk_ref[...], axis=1, keepdims=True)
        e = jnp.exp(t)
        lam = e[0:1] - e[1:2] + lam_init
        for h in heads:
            acc = acc_ref[h][0:HEAD_DIM, :]
            inv = 1.0 / acc_ref[h][HEAD_DIM:HEAD_DIM + 1, :]
            ot = acc[:, :QW] * inv[:, :QW] - lam * (acc[:, QW:] * inv[:, QW:])
            rs = lax.rsqrt(jnp.mean(ot * ot, axis=0, keepdims=True) + EPS)
            y = (ot * rs).T * g_ref[...]
            o_ref[qsl[h], hsl[h]] = (y * (1.0 - lam_init)).astype(BF16)

    def context_queries():
        stream([0])
        finish()

    def latent_queries():
        stream(range(n_latent + 1))
        finish()
        gap = functools.reduce(
            jnp.maximum, [jnp.max(smax_ref[h][...] - m_ref[h][...]) for h in heads])

        @pl.when(gap > MAX_SCORE_GAP)
        def _():
            exact_softmax()
            finish()

    if ctx_step:
        pl.when(pl.program_id(2) == 0)(context_queries)
        pl.when(pl.program_id(2) != 0)(latent_queries)
    else:
        latent_queries()


def _attention(qt, k, vt, lam_q, lam_k, g_attn, lam_init, with_ctx):
    B, nt, _, _ = qt.shape
    T = k.shape[1]
    hp = ATTN_HEADS_PER_STEP
    ctx_step = 1 if with_ctx else 0
    steps = ctx_step + (nt - 1) // ATTN_Q_TILES
    assert (nt - 1) % ATTN_Q_TILES == 0
    kern = functools.partial(_attn_kernel, lam_init=lam_init, ctx_step=ctx_step)

    def q_spec(j):
        tile = lambda i: jnp.where(i < ctx_step, j, 1 + ATTN_Q_TILES * (i - ctx_step) + j)
        return pl.BlockSpec((None, None, hp * HEAD_DIM, TILE),
                            lambda b, h, i: (b, tile(i), h, 0))

    return pl.pallas_call(
        kern,
        grid=(B, ATTN_HEADS // hp, steps),
        in_specs=[
            *[q_spec(j) for j in range(ATTN_Q_TILES)],
            pl.BlockSpec((None, T, hp * HEAD_DIM), lambda b, h, i: (b, 0, h)),
            pl.BlockSpec((None, nt, hp * HEAD_DIM, TILE), lambda b, h, i: (b, 0, h, 0)),
            pl.BlockSpec((2, QK_DIM), lambda b, h, i: (0, 0)),
            pl.BlockSpec((2, QK_DIM), lambda b, h, i: (0, 0)),
            pl.BlockSpec((1, HEAD_DIM), lambda b, h, i: (0, 0)),
        ],
        out_specs=pl.BlockSpec((None, ATTN_Q_TILES * TILE, hp * HEAD_DIM),
                               lambda b, h, i: (b, i, h)),
        out_shape=jax.ShapeDtypeStruct((B, steps * ATTN_Q_TILES * TILE, ATTN_WIDTH), BF16),
        scratch_shapes=[shape for _ in range(ATTN_STREAMS) for shape in _ATTN_HEAD_SCRATCH],
        compiler_params=_cparams(("parallel", "parallel", "arbitrary")),
        name="diff_attn",
    )(*[qt] * ATTN_Q_TILES, k, vt, lam_q, lam_k, g_attn)


def _expm1(x, u):
    near = jnp.where(u == 1.0, x, (u - 1.0) * x / jnp.log(u))
    return jnp.where(jnp.abs(x) > 0.5, u - 1.0, near)


def _group_roll(x, shift):
    n, w = x.shape
    return pltpu.roll(x.reshape(n // SUBLANES, SUBLANES, w), shift % SUBLANES, 1).reshape(n, w)


def _chunk_scan(a, bt, carry, sub, reverse):
    n = a.shape[0]
    acc_a, acc_b = a, bt
    for k in (1, 2, 4):
        if reverse:
            valid = sub <= SUBLANES - 1 - k
            shift = n - k
        else:
            valid = sub >= k
            shift = k
        a_sh = jnp.where(valid, _group_roll(acc_a, shift), 1.0)
        b_sh = jnp.where(valid, _group_roll(acc_b, shift), 0.0)
        acc_b = acc_a * b_sh + acc_b
        acc_a = acc_a * a_sh
    groups = n // SUBLANES
    order = range(groups - 1, -1, -1) if reverse else range(groups)
    edge = 0 if reverse else SUBLANES - 1
    carries = [None] * groups
    c = carry
    for g in order:
        carries[g] = jnp.broadcast_to(c, (SUBLANES, a.shape[1]))
        r = g * SUBLANES + edge
        c = acc_a[r:r + 1] * c + acc_b[r:r + 1]
    h = acc_a * jnp.concatenate(carries, axis=0) + acc_b
    return h, c


def _lru_kernel(x_ref, cw_ref, cb_ref, wg_ref, bg_ref, lam_ref, o_ref, hs_ref, conv_ref):
    T = x_ref.shape[0]
    nt = T // TILE
    row = lax.broadcasted_iota(jnp.int32, (TILE, LRU_WIDTH), 0)
    sub = row & (SUBLANES - 1)
    cw = cw_ref[...]
    cb = cb_ref[...]

    def conv(j):
        r0 = pl.multiple_of(j * TILE, TILE)
        cur = x_ref[pl.ds(r0, TILE), 0:LRU_WIDTH]
        pr = pl.multiple_of(jnp.maximum(r0 - SUBLANES, 0), SUBLANES)
        nx = pl.multiple_of(jnp.minimum(r0 + TILE, T - SUBLANES), SUBLANES)
        prev8 = x_ref[pl.ds(pr, SUBLANES), 0:LRU_WIDTH]
        next8 = x_ref[pl.ds(nx, SUBLANES), 0:LRU_WIDTH]
        has_prev = j >= 2
        has_next = jnp.logical_and(j >= 1, j <= nt - 2)
        p1 = jnp.where(has_prev, prev8[SUBLANES - 1:SUBLANES], 0.0)
        n0 = jnp.where(has_next, next8[0:1], 0.0)
        n1 = jnp.where(has_next, next8[1:2], 0.0)
        xm1 = jnp.where(row == 0, p1, pltpu.roll(cur, 1, 0))
        xp1 = jnp.where(row == TILE - 1, n0, pltpu.roll(cur, TILE - 1, 0))
        xp2 = jnp.where(row == TILE - 2, n0,
                        jnp.where(row == TILE - 1, n1, pltpu.roll(cur, TILE - 2, 0)))
        return cw[0:1] * xm1 + cw[1:2] * cur + cw[2:3] * xp1 + cw[3:4] * xp2 + cb

    def direction(j, d, carry):
        rows = pl.ds(pl.multiple_of(j * TILE, TILE), TILE)
        if d == 0:
            c = conv(j)
            conv_ref[rows, :] = c
        else:
            c = conv_ref[rows, :]
        gates = jnp.dot(c.astype(BF16), wg_ref[d], preferred_element_type=F32) + bg_ref[d]
        r = jax.nn.sigmoid(gates[:, :LRU_WIDTH])
        i = jax.nn.sigmoid(gates[:, LRU_WIDTH:])
        log_a = (-LRU_C * r) * jax.nn.softplus(-lam_ref[d])
        a = jnp.exp(log_a)
        y = -_expm1(2.0 * log_a, a * a)
        root = jnp.where(y > 0.0, y * lax.rsqrt(y), 0.0)
        bt = root * (i * c)
        return _chunk_scan(a, bt, carry, sub, reverse=(d == 1))

    def fwd(j, carry):
        h, carry = direction(j, 0, carry)
        hs_ref[pl.ds(pl.multiple_of(j * TILE, TILE), TILE), :] = h
        return carry

    def rev(n, carry):
        j = jnp.where(n == 0, 0, nt - n)
        h, carry = direction(j, 1, carry)
        r0 = pl.multiple_of(j * TILE, TILE)
        gate = x_ref[pl.ds(r0, TILE), LRU_WIDTH:]
        o_ref[pl.ds(r0, TILE), :] = ((hs_ref[pl.ds(r0, TILE), :] + h) * gate).astype(BF16)
        return carry

    zero = jnp.zeros((1, LRU_WIDTH), F32)
    lax.fori_loop(0, nt, fwd, zero)
    lax.fori_loop(0, nt, rev, zero)


def _lru(lru_in, conv_w, conv_b, w_gate, b_gate, lam):
    B, T, _ = lru_in.shape
    return pl.pallas_call(
        _lru_kernel,
        grid=(B,),
        in_specs=[
            pl.BlockSpec((None, T, 2 * LRU_WIDTH), lambda b: (b, 0, 0)),
            _const_spec((CONV_WIDTH, LRU_WIDTH)),
            _const_spec((1, LRU_WIDTH)),
            _const_spec((2, LRU_WIDTH, 2 * LRU_WIDTH)),
            _const_spec((2, 1, 2 * LRU_WIDTH)),
            _const_spec((2, 1, LRU_WIDTH)),
        ],
        out_specs=pl.BlockSpec((None, T, LRU_WIDTH), lambda b: (b, 0, 0)),
        out_shape=jax.ShapeDtypeStruct((B, T, LRU_WIDTH), BF16),
        scratch_shapes=[pltpu.VMEM((T, LRU_WIDTH), F32),
                        pltpu.VMEM((T, LRU_WIDTH), F32)],
        compiler_params=_cparams(("parallel",)),
        name="rglru",
    )(lru_in, conv_w, conv_b, w_gate, b_gate, lam)


def _ffn_kernel(*refs, t_off, n_tiles, final):
    _run_subtiles(lambda n_sub: _ffn_tiles(refs, n_sub, t_off, final), t_off, n_tiles)


def _ffn_tiles(refs, n_sub, t_off, final):
    per_sub = 5
    subs = [refs[per_sub * j:per_sub * (j + 1)] for j in range(n_sub)]
    xc_ref, g2_ref, gf_ref, wo_ref, wg_ref, wu_ref, wd_ref, o_ref = refs[per_sub * SUBTILES:]
    first_tile = SUBTILES * pl.program_id(1) + t_off

    proj = [jnp.dot(jnp.concatenate([ya[...], yr[...], ys[...]], axis=-1), wo_ref[...],
                    preferred_element_type=F32) for ya, yr, ys, _, _ in subs]
    x1, g, u = [], [], []
    for j, (_, _, _, xl_ref, mod_ref) in enumerate(subs):
        x = jnp.where(first_tile + j == 0, xc_ref[...], xl_ref[...])
        x1.append(x + mod_ref[2] * proj[j])
        rs = lax.rsqrt(jnp.mean(x1[j] * x1[j], axis=-1, keepdims=True) + EPS)
        h = ((x1[j] * rs * g2_ref[...]) * (1.0 + mod_ref[4]) + mod_ref[3]).astype(BF16)
        g.append(jnp.dot(h, wg_ref[...], preferred_element_type=F32))
        u.append(jnp.dot(h, wu_ref[...], preferred_element_type=F32))
    for j, (_, _, _, _, mod_ref) in enumerate(subs):
        act = ((g[j] * jax.nn.sigmoid(g[j])) * u[j]).astype(BF16)
        x2 = x1[j] + mod_ref[5] * jnp.dot(act, wd_ref[...], preferred_element_type=F32)
        if final:
            x2 = x2 * lax.rsqrt(jnp.mean(x2 * x2, axis=-1, keepdims=True) + EPS) * gf_ref[...]
        o_ref[j * TILE:(j + 1) * TILE, :] = x2


def _ffn(ya, yr, ys, x_ctx, x_lat, lat_first, mod, g2, g_final, layer, w_out, w_gate, w_up,
         w_down, t_off, final):
    B, T, _ = yr.shape
    n_tiles = T // TILE
    nt = n_tiles - t_off
    with_ctx = t_off == 0
    ya_block = ((lambda t: jnp.where(t == 0, 0, ATTN_Q_TILES + t - 1)) if with_ctx
                else (lambda t: t - 1))
    kern = functools.partial(_ffn_kernel, t_off=t_off, n_tiles=n_tiles, final=final)

    def sub_specs(j):
        tile = lambda i: _sub_tile(i, j, t_off, n_tiles)
        return [pl.BlockSpec((None, TILE, ATTN_WIDTH), lambda b, i: (b, ya_block(tile(i)), 0)),
                _token_spec(tile, LRU_WIDTH),
                _token_spec(tile, SGU_WIDTH), _latent_spec(tile, lat_first), _mod_spec(tile)]

    tokens = []
    for j in range(SUBTILES):
        tokens += [ya, yr, ys, x_lat, mod]
    return pl.pallas_call(
        kern,
        grid=(B, pl.cdiv(nt, SUBTILES)),
        in_specs=[
            *[spec for j in range(SUBTILES) for spec in sub_specs(j)],
            _CTX_SPEC,
            _const_spec((1, D_MODEL)),
            _const_spec((1, D_MODEL)),
            _layer_spec(layer, (D_MODEL, D_MODEL)),
            _layer_spec(layer, (D_MODEL, FFN_HIDDEN)),
            _layer_spec(layer, (D_MODEL, FFN_HIDDEN)),
            _layer_spec(layer, (FFN_HIDDEN, D_MODEL)),
        ],
        out_specs=pl.BlockSpec((None, SUBTILES * TILE, D_MODEL), lambda b, i: (b, i, 0)),
        out_shape=jax.ShapeDtypeStruct((B, nt * TILE, D_MODEL), F32),
        compiler_params=_cparams(("parallel", "parallel")),
        name="outproj_ffn",
    )(*tokens, x_ctx, g2, g_final, w_out, w_gate, w_up, w_down)


def _rope_tables(seq):
    rows = seq // GRID_W
    row = jnp.repeat(jnp.arange(rows), GRID_W).astype(F32)
    col = jnp.tile(jnp.arange(GRID_W), rows).astype(F32)
    n = ROPE_AXIS_DIM // 2
    inv = ROPE_THETA ** (-jnp.arange(n, dtype=F32) / n)
    ang_r = row[:, None] * inv[None, :]
    ang_c = col[:, None] * inv[None, :]
    cr, sr, cc, sc = jnp.cos(ang_r), jnp.sin(ang_r), jnp.cos(ang_c), jnp.sin(ang_c)
    cos = jnp.tile(jnp.concatenate([cr, cr, cc, cc], axis=1), (1, 2))
    sin = jnp.tile(jnp.concatenate([-sr, sr, -sc, sc], axis=1), (1, 2))
    cos = jnp.concatenate([jnp.ones((CTX_LEN, HEAD_DIM), F32), cos], axis=0)
    sin = jnp.concatenate([jnp.zeros((CTX_LEN, HEAD_DIM), F32), sin], axis=0)
    return cos, sin


def _block_diag(w):
    eye = jnp.eye(w.shape[0], dtype=w.dtype)
    return jnp.einsum('hij,hk->hikj', w, eye).reshape(w.shape[0] * w.shape[1],
                                                      w.shape[0] * w.shape[2])


def kernel(x, c, ctx, c_ctx, w_ada, b_ada, g_norm1, g_norm2, w_in, lam_q, lam_k, g_attn, conv_w,
           conv_b, w_rg_a, b_rg_a, w_rg_x, b_rg_x, lru_lambda, g_sgu, w_spatial, b_spatial, w_out,
           w_ffn_gate, w_ffn_up, w_ffn_down, g_final):
    B, S, D = x.shape
    assert (D, ctx.shape[1], S % KEY_CHUNK) == (D_MODEL, CTX_LEN, 0) and B <= 8
    cos_t, sin_t = _rope_tables(S)
    cs = jnp.zeros((16, D), F32).at[:B].set(c).at[8].set(c_ctx)
    mods = _ada(cs, w_ada, b_ada)
    ones_bd = _block_diag(jnp.ones((SGU_GROUPS, SGU_GROUP_DIM, SGU_GROUP_DIM), BF16))
    w_in16, w_out16 = w_in.astype(BF16), w_out.astype(BF16)
    w_gate16, w_up16, w_down16 = (w.astype(BF16) for w in (w_ffn_gate, w_ffn_up, w_ffn_down))
    stream = (ctx, x, 1)

    for l in range(DEPTH):
        last = l == DEPTH - 1
        lam_init = 0.8 - 0.6 * math.exp(-0.3 * l)
        mod = mods[l].reshape(16, N_MOD, 1, D)
        w_sp = jnp.transpose(w_spatial[l], (1, 0, 2)).reshape(CHUNK, SGU_GROUPS * CHUNK)
        b_sp = jnp.repeat(b_spatial[l].T, SGU_GROUP_DIM, axis=1)
        w_gate = jnp.stack([jnp.concatenate([_block_diag(w_rg_a[l, d]), _block_diag(w_rg_x[l, d])],
                                            axis=1) for d in range(2)]).astype(BF16)
        b_gate = jnp.concatenate([b_rg_a[l], b_rg_x[l]], axis=1)[:, None, :]

        qt, k, vt, lru_in, ys = _inproj(
            *stream, mod, g_norm1[l][None], l, w_in16, cos_t, sin_t, g_sgu[l][None],
            ones_bd, w_sp.astype(BF16), b_sp)
        t_off = 1 if last else 0
        ya = _attention(qt, k, vt, lam_q[l], lam_k[l], g_attn[l][None], lam_init, not last)
        yr = _lru(lru_in, conv_w[l], conv_b[l][None], w_gate, b_gate, lru_lambda[l][:, None, :])
        xs = _ffn(ya, yr, ys, *stream, mod, g_norm2[l][None], g_final[None], l, w_out16,
                  w_gate16, w_up16, w_down16, t_off, last)
        stream = (xs, xs, 0)
    return xs
```

```python
import functools
import math

import jax
import jax.numpy as jnp
from jax import lax
from jax.experimental import pallas as pl
from jax.experimental.pallas import tpu as pltpu

F32 = jnp.float32
BF16 = jnp.bfloat16

D_MODEL = 1024
DEPTH = 2
GRID_W = 64
CTX_LEN = 256
N_MOD = 6
EPS = 1e-6
ATTN_WIDTH = 512
LRU_WIDTH = 256
SGU_WIDTH = 256
ATTN_HEADS = 4
ATTN_HEADS_PER_STEP = 4
HEAD_DIM = 128
QK_DIM = 64
ROPE_AXIS_DIM = 32
ROPE_THETA = 10000.0
LRU_HEADS = 4
LRU_BLOCK = 64
CONV_WIDTH = 4
LRU_C = 8.0
SGU_GROUPS = 4
SGU_GROUP_DIM = 64
CHUNK = 128
FFN_HIDDEN = 2816
PROJ_WIDTH = 2560

TILE = 256
SUBTILES = 2
KEY_CHUNK = TILE
SUBLANES = 8
VMEM_LIMIT = 56 * 1024 * 1024


def _cparams(sem, flags=None):
    return pltpu.CompilerParams(dimension_semantics=sem, vmem_limit_bytes=VMEM_LIMIT, flags=flags)


def _const_spec(shape):
    nd = len(shape)
    return pl.BlockSpec(shape, lambda *_: (0,) * nd, pipeline_mode=pl.Buffered(1))


def _layer_spec(layer, shape):
    nd = len(shape)
    return pl.BlockSpec((None, *shape), lambda *_: (layer,) + (0,) * nd,
                        pipeline_mode=pl.Buffered(1))


def _sub_tile(i, j, t_off, n_tiles):
    return jnp.minimum(SUBTILES * i + j + t_off, n_tiles - 1)


def _token_spec(tile, width, first=0):
    return pl.BlockSpec((None, TILE, width), lambda b, i: (b, tile(i) - first, 0))


def _latent_spec(tile, lat_first):
    return pl.BlockSpec((None, TILE, D_MODEL),
                        lambda b, i: (b, jnp.maximum(tile(i), 1) - lat_first, 0))


def _mod_spec(tile):
    return pl.BlockSpec((None, N_MOD, 1, D_MODEL),
                        lambda b, i: (jnp.where(tile(i) == 0, 8, b), 0, 0, 0))


_CTX_SPEC = pl.BlockSpec((None, TILE, D_MODEL), lambda b, i: (b, 0, 0))


def _run_subtiles(body, t_off, n_tiles):
    rem = (n_tiles - t_off) % SUBTILES
    if rem == 0:
        body(SUBTILES)
    else:
        full = SUBTILES * (pl.program_id(1) + 1) + t_off <= n_tiles
        pl.when(full)(lambda: body(SUBTILES))
        pl.when(jnp.logical_not(full))(lambda: body(rem))


def _ada_kernel(c_ref, w_ref, b_ref, o_ref):
    c = c_ref[...]
    h = (c * jax.nn.sigmoid(c)).astype(BF16)
    o_ref[...] = jnp.dot(h, w_ref[...].astype(BF16), preferred_element_type=F32) + b_ref[...]


def _ada(cs, w_ada, b_ada):
    tn = 1536
    n_out = N_MOD * D_MODEL
    return pl.pallas_call(
        _ada_kernel,
        grid=(DEPTH, n_out // tn),
        in_specs=[
            pl.BlockSpec((16, D_MODEL), lambda l, n: (0, 0)),
            pl.BlockSpec((None, D_MODEL, tn), lambda l, n: (l, 0, n)),
            pl.BlockSpec((None, 1, tn), lambda l, n: (l, 0, n)),
        ],
        out_specs=pl.BlockSpec((None, 16, tn), lambda l, n: (l, 0, n)),
        out_shape=jax.ShapeDtypeStruct((DEPTH, 16, n_out), F32),
        compiler_params=_cparams(("arbitrary", "arbitrary")),
        name="ada_mod",
    )(cs, w_ada, b_ada.reshape(DEPTH, 1, n_out))


def _rope(x, cos, sin, first_half):
    partner = jnp.where(first_half, pltpu.roll(x, 128 - 16, 1), pltpu.roll(x, 16, 1))
    return x * cos + partner * sin


def _inproj_kernel(*refs, n_tiles):
    _run_subtiles(lambda n_sub: _inproj_tiles(refs, n_sub), 0, n_tiles)


def _inproj_tiles(refs, n_sub):
    per_sub = 4
    subs = [refs[per_sub * j:per_sub * (j + 1)] for j in range(n_sub)]
    (xc_ref, g1_ref, w_ref, gs_ref, ones_ref, wsp_ref, bsp_ref,
     qt_ref, k_ref, vt_ref, lru_ref, sgu_ref) = refs[per_sub * SUBTILES:]
    first_tile = SUBTILES * pl.program_id(1)

    p_all = []
    for j, (xl_ref, mod_ref, _, _) in enumerate(subs):
        x = jnp.where(first_tile + j == 0, xc_ref[...], xl_ref[...])
        rs = lax.rsqrt(jnp.mean(x * x, axis=-1, keepdims=True) + EPS)
        h = (x * rs * g1_ref[...]) * (1.0 + mod_ref[1]) + mod_ref[0]
        p_all.append(jnp.dot(h.astype(BF16), w_ref[...], preferred_element_type=F32))

    lane = lax.broadcasted_iota(jnp.int32, (TILE, HEAD_DIM), 1)
    first_half = (lane & 16) == 0
    grp = lax.broadcasted_iota(jnp.int32, (CHUNK, SGU_WIDTH), 1) // SGU_GROUP_DIM
    scale = QK_DIM ** -0.5 * math.log2(math.e)
    for j, (_, _, cos_ref, sin_ref) in enumerate(subs):
        p = p_all[j]
        rows = slice(j * TILE, (j + 1) * TILE)
        cos = cos_ref[...]
        sin = sin_ref[...]
        q_parts, k_parts = [], []
        for hd in range(ATTN_HEADS):
            q = p[:, hd * HEAD_DIM:(hd + 1) * HEAD_DIM]
            k = p[:, ATTN_WIDTH + hd * HEAD_DIM:ATTN_WIDTH + (hd + 1) * HEAD_DIM]
            q_parts.append(_rope(q, cos, sin, first_half) * scale)
            k_parts.append(_rope(k, cos, sin, first_half))
        q = jnp.concatenate(q_parts, axis=1)
        qt_ref[j] = q.T.astype(BF16)
        k_ref[rows, :] = jnp.concatenate(k_parts, axis=1).astype(BF16)
        vt_ref[j] = p[:, 2 * ATTN_WIDTH:3 * ATTN_WIDTH].T.astype(BF16)

        lx0 = 3 * ATTN_WIDTH
        lru_ref[rows, 0:LRU_WIDTH] = p[:, lx0:lx0 + LRU_WIDTH]
        lru_ref[rows, LRU_WIDTH:] = jax.nn.gelu(p[:, lx0 + LRU_WIDTH:lx0 + 2 * LRU_WIDTH])

        su0 = lx0 + 2 * LRU_WIDTH
        u = jax.nn.gelu(p[:, su0:su0 + SGU_WIDTH])
        v = jax.nn.gelu(p[:, su0 + SGU_WIDTH:su0 + 2 * SGU_WIDTH])
        ss = v * v
        hi = ss.astype(BF16)
        lo = (ss - hi.astype(F32)).astype(BF16)
        gsum = (jnp.dot(hi, ones_ref[...], preferred_element_type=F32)
                + jnp.dot(lo, ones_ref[...], preferred_element_type=F32))
        vn = v * lax.rsqrt(gsum * (1.0 / SGU_GROUP_DIM) + EPS) * gs_ref[...]
        for c in range(TILE // CHUNK):
            vc = vn[c * CHUNK:(c + 1) * CHUNK]
            rhs = jnp.concatenate(
                [jnp.where(grp == g, vc, 0.0) for g in range(SGU_GROUPS)], axis=0).astype(BF16)
            m = jnp.dot(wsp_ref[...], rhs, preferred_element_type=F32) + bsp_ref[...]
            r0 = j * TILE + c * CHUNK
            sgu_ref[r0:r0 + CHUNK, :] = (u[c * CHUNK:(c + 1) * CHUNK] * m).astype(BF16)


def _inproj(x_ctx, x_lat, lat_first, mod, g1, layer, w_in, cos_t, sin_t, g_sgu, ones_bd, w_sp,
            b_sp):
    B = x_lat.shape[0]
    T = cos_t.shape[0]
    nt = T // TILE

    def sub_specs(j):
        tile = lambda i: _sub_tile(i, j, 0, nt)
        table = pl.BlockSpec((TILE, HEAD_DIM), lambda b, i: (tile(i), 0))
        return [_latent_spec(tile, lat_first), _mod_spec(tile), table, table]

    tokens = []
    for j in range(SUBTILES):
        tokens += [x_lat, mod, cos_t, sin_t]
    rows = SUBTILES * TILE
    return pl.pallas_call(
        functools.partial(_inproj_kernel, n_tiles=nt),
        grid=(B, pl.cdiv(nt, SUBTILES)),
        in_specs=[
            *[spec for j in range(SUBTILES) for spec in sub_specs(j)],
            _CTX_SPEC,
            _const_spec((1, D_MODEL)),
            _layer_spec(layer, (D_MODEL, PROJ_WIDTH)),
            _const_spec((1, SGU_WIDTH)),
            _const_spec((SGU_WIDTH, SGU_WIDTH)),
            _const_spec((CHUNK, SGU_GROUPS * CHUNK)),
            _const_spec((CHUNK, SGU_WIDTH)),
        ],
        out_specs=[
            pl.BlockSpec((None, SUBTILES, ATTN_WIDTH, TILE), lambda b, i: (b, i, 0, 0)),
            pl.BlockSpec((None, rows, ATTN_WIDTH), lambda b, i: (b, i, 0)),
            pl.BlockSpec((None, SUBTILES, ATTN_WIDTH, TILE), lambda b, i: (b, i, 0, 0)),
            pl.BlockSpec((None, rows, 2 * LRU_WIDTH), lambda b, i: (b, i, 0)),
            pl.BlockSpec((None, rows, SGU_WIDTH), lambda b, i: (b, i, 0)),
        ],
        out_shape=[
            jax.ShapeDtypeStruct((B, nt, ATTN_WIDTH, TILE), BF16),
            jax.ShapeDtypeStruct((B, T, ATTN_WIDTH), BF16),
            jax.ShapeDtypeStruct((B, nt, ATTN_WIDTH, TILE), BF16),
            jax.ShapeDtypeStruct((B, T, 2 * LRU_WIDTH), F32),
            jax.ShapeDtypeStruct((B, T, SGU_WIDTH), BF16),
        ],
        compiler_params=_cparams(("parallel", "parallel")),
        name="inproj",
    )(*tokens, x_ctx, g1, w_in, g_sgu, ones_bd, w_sp, b_sp)


MAX_SCORE_GAP = 60.0

BF16_ROWS = 16
ATTN_Q_TILES = 1
ATTN_STREAMS = ATTN_Q_TILES * ATTN_HEADS_PER_STEP
QW = TILE
_ATTN_HEAD_SCRATCH = [
    pltpu.VMEM((HEAD_DIM + BF16_ROWS, 2 * QW), F32),
    pltpu.VMEM((1, 2 * QW), F32),
    pltpu.VMEM((SUBLANES, 2 * QW), F32),
    pltpu.VMEM((HEAD_DIM, 2 * QW), BF16)]


def _attn_kernel(*refs, lam_init, ctx_step):
    q_refs = refs[:ATTN_Q_TILES]
    k_ref, v_ref, lq_ref, lk_ref, g_ref, o_ref = refs[ATTN_Q_TILES:ATTN_Q_TILES + 6]
    scratch = refs[ATTN_Q_TILES + 6:]
    heads = range(ATTN_STREAMS)
    per_head = len(_ATTN_HEAD_SCRATCH)
    acc_ref, m_ref, smax_ref, rhs_ref = [scratch[n::per_head] for n in range(per_head)]
    hsl = [slice((h % ATTN_HEADS_PER_STEP) * HEAD_DIM, (h % ATTN_HEADS_PER_STEP + 1) * HEAD_DIM)
           for h in heads]
    qsl = [slice((h // ATTN_HEADS_PER_STEP) * TILE, (h // ATTN_HEADS_PER_STEP + 1) * TILE)
           for h in heads]
    row = lax.broadcasted_iota(jnp.int32, (HEAD_DIM, QW), 0)
    for h in heads:
        qf = q_refs[h // ATTN_HEADS_PER_STEP][hsl[h], :].astype(F32)
        rhs_ref[h][...] = jnp.concatenate([jnp.where(row < QK_DIM, qf, 0.0),
                                           jnp.where(row >= QK_DIM, qf, 0.0)],
                                          axis=1).astype(BF16)

    n_latent = (k_ref.shape[0] - CTX_LEN) // KEY_CHUNK

    def scores(h, c):
        if isinstance(c, int) and c == 0:
            rows = slice(0, CTX_LEN)
        else:
            rows = pl.ds(pl.multiple_of(CTX_LEN + (c - 1) * KEY_CHUNK, TILE), KEY_CHUNK)
        return jnp.dot(k_ref[rows, hsl[h]], rhs_ref[h][...], preferred_element_type=F32)

    def exponentiate(s, m):
        return jnp.exp2(s - m).astype(BF16)

    def accumulate(h, c, p):
        ctx = isinstance(c, int) and c == 0
        r = KEY_CHUNK // TILE
        v = (v_ref[0, hsl[h], :] if ctx else
             jnp.concatenate([v_ref[r * (c - 1) + 1 + t, hsl[h], :] for t in range(r)], axis=1))
        v = jnp.concatenate([v, jnp.ones((BF16_ROWS, v.shape[1]), BF16)], axis=0)
        upd = jnp.dot(v, p, preferred_element_type=F32)
        acc_ref[h][...] = upd if ctx else acc_ref[h][...] + upd

    def probabilities(h, c):
        s = scores(h, c)
        if c == 0:
            m_ref[h][...] = jnp.max(s, axis=0, keepdims=True)
        else:
            smax = jnp.max(s.reshape(KEY_CHUNK // SUBLANES, SUBLANES, 2 * QW), axis=0)
            smax_ref[h][...] = smax if c == 1 else jnp.maximum(smax_ref[h][...], smax)
        return exponentiate(s, m_ref[h][...])

    def stream(chunks):
        items = [(h, c) for c in chunks for h in heads]
        skew = 2
        ps = [probabilities(*item) for item in items[:skew]]
        for i, (h, c) in enumerate(items):
            if i + skew < len(items):
                ps.append(probabilities(*items[i + skew]))
            accumulate(h, c, ps[i])
            ps[i] = None

    def exact_softmax():
        for h in heads:
            m = jnp.maximum(m_ref[h][...], jnp.max(smax_ref[h][...], axis=0, keepdims=True))
            accumulate(h, 0, exponentiate(scores(h, 0), m))

            def chunk(c, carry, h=h, m=m):
                accumulate(h, c, exponentiate(scores(h, c), m))
                return carry

            lax.fori_loop(1, n_latent + 1, chunk, 0)

    def finish():
        t = jnp.sum(lq_ref[...] * lk_ref[...], axis=1, keepdims=True)
        e = jnp.exp(t)
        lam = e[0:1] - e[1:2] + lam_init
        for h in heads:
            acc = acc_ref[h][0:HEAD_DIM, :]
            inv = 1.0 / acc_ref[h][HEAD_DIM:HEAD_DIM + 1, :]
            ot = acc[:, :QW] * inv[:, :QW] - lam * (acc[:, QW:] * inv[:, QW:])
            rs = lax.rsqrt(jnp.mean(ot * ot, axis=0, keepdims=True) + EPS)
            y = (ot * rs).T * g_ref[...]
            o_ref[qsl[h], hsl[h]] = (y * (1.0 - lam_init)).astype(BF16)

    def context_queries():
        stream([0])
        finish()

    def latent_queries():
        stream(range(n_latent + 1))
        finish()
        gap = functools.reduce(
            jnp.maximum, [jnp.max(smax_ref[h][...] - m_ref[h][...]) for h in heads])

        @pl.when(gap > MAX_SCORE_GAP)
        def _():
            exact_softmax()
            finish()

    if ctx_step:
        pl.when(pl.program_id(2) == 0)(context_queries)
        pl.when(pl.program_id(2) != 0)(latent_queries)
    else:
        latent_queries()


def _attention(qt, k, vt, lam_q, lam_k, g_attn, lam_init, with_ctx):
    B, nt, _, _ = qt.shape
    T = k.shape[1]
    hp = ATTN_HEADS_PER_STEP
    ctx_step = 1 if with_ctx else 0
    steps = ctx_step + (nt - 1) // ATTN_Q_TILES
    assert (nt - 1) % ATTN_Q_TILES == 0
    kern = functools.partial(_attn_kernel, lam_init=lam_init, ctx_step=ctx_step)

    def q_spec(j):
        tile = lambda i: jnp.where(i < ctx_step, j, 1 + ATTN_Q_TILES * (i - ctx_step) + j)
        return pl.BlockSpec((None, None, hp * HEAD_DIM, TILE),
                            lambda b, h, i: (b, tile(i), h, 0))

    return pl.pallas_call(
        kern,
        grid=(B, ATTN_HEADS // hp, steps),
        in_specs=[
            *[q_spec(j) for j in range(ATTN_Q_TILES)],
            pl.BlockSpec((None, T, hp * HEAD_DIM), lambda b, h, i: (b, 0, h)),
            pl.BlockSpec((None, nt, hp * HEAD_DIM, TILE), lambda b, h, i: (b, 0, h, 0)),
            pl.BlockSpec((2, QK_DIM), lambda b, h, i: (0, 0)),
            pl.BlockSpec((2, QK_DIM), lambda b, h, i: (0, 0)),
            pl.BlockSpec((1, HEAD_DIM), lambda b, h, i: (0, 0)),
        ],
        out_specs=pl.BlockSpec((None, ATTN_Q_TILES * TILE, hp * HEAD_DIM),
                               lambda b, h, i: (b, i, h)),
        out_shape=jax.ShapeDtypeStruct((B, steps * ATTN_Q_TILES * TILE, ATTN_WIDTH), BF16),
        scratch_shapes=[shape for _ in range(ATTN_STREAMS) for shape in _ATTN_HEAD_SCRATCH],
        compiler_params=_cparams(("parallel", "parallel", "arbitrary")),
        name="diff_attn",
    )(*[qt] * ATTN_Q_TILES, k, vt, lam_q, lam_k, g_attn)


def _expm1(x, u):
    near = jnp.where(u == 1.0, x, (u - 1.0) * x / jnp.log(u))
    return jnp.where(jnp.abs(x) > 0.5, u - 1.0, near)


def _group_roll(x, shift):
    n, w = x.shape
    return pltpu.roll(x.reshape(n // SUBLANES, SUBLANES, w), shift % SUBLANES, 1).reshape(n, w)


def _chunk_scan(a, bt, carry, sub, reverse):
    n = a.shape[0]
    acc_a, acc_b = a, bt
    for k in (1, 2, 4):
        if reverse:
            valid = sub <= SUBLANES - 1 - k
            shift = n - k
        else:
            valid = sub >= k
            shift = k
        a_sh = jnp.where(valid, _group_roll(acc_a, shift), 1.0)
        b_sh = jnp.where(valid, _group_roll(acc_b, shift), 0.0)
        acc_b = acc_a * b_sh + acc_b
        acc_a = acc_a * a_sh
    groups = n // SUBLANES
    order = range(groups - 1, -1, -1) if reverse else range(groups)
    edge = 0 if reverse else SUBLANES - 1
    carries = [None] * groups
    c = carry
    for g in order:
        carries[g] = jnp.broadcast_to(c, (SUBLANES, a.shape[1]))
        r = g * SUBLANES + edge
        c = acc_a[r:r + 1] * c + acc_b[r:r + 1]
    h = acc_a * jnp.concatenate(carries, axis=0) + acc_b
    return h, c


def _lru_kernel(x_ref, cw_ref, cb_ref, wg_ref, bg_ref, lam_ref, o_ref, hs_ref, conv_ref):
    T = x_ref.shape[0]
    nt = T // TILE
    row = lax.broadcasted_iota(jnp.int32, (TILE, LRU_WIDTH), 0)
    sub = row & (SUBLANES - 1)
    cw = cw_ref[...]
    cb = cb_ref[...]

    def conv(j):
        r0 = pl.multiple_of(j * TILE, TILE)
        cur = x_ref[pl.ds(r0, TILE), 0:LRU_WIDTH]
        pr = pl.multiple_of(jnp.maximum(r0 - SUBLANES, 0), SUBLANES)
        nx = pl.multiple_of(jnp.minimum(r0 + TILE, T - SUBLANES), SUBLANES)
        prev8 = x_ref[pl.ds(pr, SUBLANES), 0:LRU_WIDTH]
        next8 = x_ref[pl.ds(nx, SUBLANES), 0:LRU_WIDTH]
        has_prev = j >= 2
        has_next = jnp.logical_and(j >= 1, j <= nt - 2)
        p1 = jnp.where(has_prev, prev8[SUBLANES - 1:SUBLANES], 0.0)
        n0 = jnp.where(has_next, next8[0:1], 0.0)
        n1 = jnp.where(has_next, next8[1:2], 0.0)
        xm1 = jnp.where(row == 0, p1, pltpu.roll(cur, 1, 0))
        xp1 = jnp.where(row == TILE - 1, n0, pltpu.roll(cur, TILE - 1, 0))
        xp2 = jnp.where(row == TILE - 2, n0,
                        jnp.where(row == TILE - 1, n1, pltpu.roll(cur, TILE - 2, 0)))
        return cw[0:1] * xm1 + cw[1:2] * cur + cw[2:3] * xp1 + cw[3:4] * xp2 + cb

    def direction(j, d, carry):
        rows = pl.ds(pl.multiple_of(j * TILE, TILE), TILE)
        if d == 0:
            c = conv(j)
            conv_ref[rows, :] = c
        else:
            c = conv_ref[rows, :]
        gates = jnp.dot(c.astype(BF16), wg_ref[d], preferred_element_type=F32) + bg_ref[d]
        r = jax.nn.sigmoid(gates[:, :LRU_WIDTH])
        i = jax.nn.sigmoid(gates[:, LRU_WIDTH:])
        log_a = (-LRU_C * r) * jax.nn.softplus(-lam_ref[d])
        a = jnp.exp(log_a)
        y = -_expm1(2.0 * log_a, a * a)
        root = jnp.where(y > 0.0, y * lax.rsqrt(y), 0.0)
        bt = root * (i * c)
        return _chunk_scan(a, bt, carry, sub, reverse=(d == 1))

    def fwd(j, carry):
        h, carry = direction(j, 0, carry)
        hs_ref[pl.ds(pl.multiple_of(j * TILE, TILE), TILE), :] = h
        return carry

    def rev(n, carry):
        j = jnp.where(n == 0, 0, nt - n)
        h, carry = direction(j, 1, carry)
        r0 = pl.multiple_of(j * TILE, TILE)
        gate = x_ref[pl.ds(r0, TILE), LRU_WIDTH:]
        o_ref[pl.ds(r0, TILE), :] = ((hs_ref[pl.ds(r0, TILE), :] + h) * gate).astype(BF16)
        return carry

    zero = jnp.zeros((1, LRU_WIDTH), F32)
    lax.fori_loop(0, nt, fwd, zero)
    lax.fori_loop(0, nt, rev, zero)


def _lru(lru_in, conv_w, conv_b, w_gate, b_gate, lam):
    B, T, _ = lru_in.shape
    return pl.pallas_call(
        _lru_kernel,
        grid=(B,),
        in_specs=[
            pl.BlockSpec((None, T, 2 * LRU_WIDTH), lambda b: (b, 0, 0)),
            _const_spec((CONV_WIDTH, LRU_WIDTH)),
            _const_spec((1, LRU_WIDTH)),
            _const_spec((2, LRU_WIDTH, 2 * LRU_WIDTH)),
            _const_spec((2, 1, 2 * LRU_WIDTH)),
            _const_spec((2, 1, LRU_WIDTH)),
        ],
        out_specs=pl.BlockSpec((None, T, LRU_WIDTH), lambda b: (b, 0, 0)),
        out_shape=jax.ShapeDtypeStruct((B, T, LRU_WIDTH), BF16),
        scratch_shapes=[pltpu.VMEM((T, LRU_WIDTH), F32),
                        pltpu.VMEM((T, LRU_WIDTH), F32)],
        compiler_params=_cparams(("parallel",)),
        name="rglru",
    )(lru_in, conv_w, conv_b, w_gate, b_gate, lam)


def _ffn_kernel(*refs, t_off, n_tiles, final):
    _run_subtiles(lambda n_sub: _ffn_tiles(refs, n_sub, t_off, final), t_off, n_tiles)


def _ffn_tiles(refs, n_sub, t_off, final):
    per_sub = 5
    subs = [refs[per_sub * j:per_sub * (j + 1)] for j in range(n_sub)]
    xc_ref, g2_ref, gf_ref, wo_ref, wg_ref, wu_ref, wd_ref, o_ref = refs[per_sub * SUBTILES:]
    first_tile = SUBTILES * pl.program_id(1) + t_off

    proj = [jnp.dot(jnp.concatenate([ya[...], yr[...], ys[...]], axis=-1), wo_ref[...],
                    preferred_element_type=F32) for ya, yr, ys, _, _ in subs]
    x1, g, u = [], [], []
    for j, (_, _, _, xl_ref, mod_ref) in enumerate(subs):
        x = jnp.where(first_tile + j == 0, xc_ref[...], xl_ref[...])
        x1.append(x + mod_ref[2] * proj[j])
        rs = lax.rsqrt(jnp.mean(x1[j] * x1[j], axis=-1, keepdims=True) + EPS)
        h = ((x1[j] * rs * g2_ref[...]) * (1.0 + mod_ref[4]) + mod_ref[3]).astype(BF16)
        g.append(jnp.dot(h, wg_ref[...], preferred_element_type=F32))
        u.append(jnp.dot(h, wu_ref[...], preferred_element_type=F32))
    for j, (_, _, _, _, mod_ref) in enumerate(subs):
        act = ((g[j] * jax.nn.sigmoid(g[j])) * u[j]).astype(BF16)
        x2 = x1[j] + mod_ref[5] * jnp.dot(act, wd_ref[...], preferred_element_type=F32)
        if final:
            x2 = x2 * lax.rsqrt(jnp.mean(x2 * x2, axis=-1, keepdims=True) + EPS) * gf_ref[...]
        o_ref[j * TILE:(j + 1) * TILE, :] = x2


def _ffn(ya, yr, ys, x_ctx, x_lat, lat_first, mod, g2, g_final, layer, w_out, w_gate, w_up,
         w_down, t_off, final):
    B, T, _ = yr.shape
    n_tiles = T // TILE
    nt = n_tiles - t_off
    with_ctx = t_off == 0
    ya_block = ((lambda t: jnp.where(t == 0, 0, ATTN_Q_TILES + t - 1)) if with_ctx
                else (lambda t: t - 1))
    kern = functools.partial(_ffn_kernel, t_off=t_off, n_tiles=n_tiles, final=final)

    def sub_specs(j):
        tile = lambda i: _sub_tile(i, j, t_off, n_tiles)
        return [pl.BlockSpec((None, TILE, ATTN_WIDTH), lambda b, i: (b, ya_block(tile(i)), 0)),
                _token_spec(tile, LRU_WIDTH),
                _token_spec(tile, SGU_WIDTH), _latent_spec(tile, lat_first), _mod_spec(tile)]

    tokens = []
    for j in range(SUBTILES):
        tokens += [ya, yr, ys, x_lat, mod]
    return pl.pallas_call(
        kern,
        grid=(B, pl.cdiv(nt, SUBTILES)),
        in_specs=[
            *[spec for j in range(SUBTILES) for spec in sub_specs(j)],
            _CTX_SPEC,
            _const_spec((1, D_MODEL)),
            _const_spec((1, D_MODEL)),
            _layer_spec(layer, (D_MODEL, D_MODEL)),
            _layer_spec(layer, (D_MODEL, FFN_HIDDEN)),
            _layer_spec(layer, (D_MODEL, FFN_HIDDEN)),
            _layer_spec(layer, (FFN_HIDDEN, D_MODEL)),
        ],
        out_specs=pl.BlockSpec((None, SUBTILES * TILE, D_MODEL), lambda b, i: (b, i, 0)),
        out_shape=jax.ShapeDtypeStruct((B, nt * TILE, D_MODEL), F32),
        compiler_params=_cparams(("parallel", "parallel")),
        name="outproj_ffn",
    )(*tokens, x_ctx, g2, g_final, w_out, w_gate, w_up, w_down)


def _rope_tables(seq):
    rows = seq // GRID_W
    row = jnp.repeat(jnp.arange(rows), GRID_W).astype(F32)
    col = jnp.tile(jnp.arange(GRID_W), rows).astype(F32)
    n = ROPE_AXIS_DIM // 2
    inv = ROPE_THETA ** (-jnp.arange(n, dtype=F32) / n)
    ang_r = row[:, None] * inv[None, :]
    ang_c = col[:, None] * inv[None, :]
    cr, sr, cc, sc = jnp.cos(ang_r), jnp.sin(ang_r), jnp.cos(ang_c), jnp.sin(ang_c)
    cos = jnp.tile(jnp.concatenate([cr, cr, cc, cc], axis=1), (1, 2))
    sin = jnp.tile(jnp.concatenate([-sr, sr, -sc, sc], axis=1), (1, 2))
    cos = jnp.concatenate([jnp.ones((CTX_LEN, HEAD_DIM), F32), cos], axis=0)
    sin = jnp.concatenate([jnp.zeros((CTX_LEN, HEAD_DIM), F32), sin], axis=0)
    return cos, sin


def _block_diag(w):
    eye = jnp.eye(w.shape[0], dtype=w.dtype)
    return jnp.einsum('hij,hk->hikj', w, eye).reshape(w.shape[0] * w.shape[1],
                                                      w.shape[0] * w.shape[2])


def kernel(x, c, ctx, c_ctx, w_ada, b_ada, g_norm1, g_norm2, w_in, lam_q, lam_k, g_attn, conv_w,
           conv_b, w_rg_a, b_rg_a, w_rg_x, b_rg_x, lru_lambda, g_sgu, w_spatial, b_spatial, w_out,
           w_ffn_gate, w_ffn_up, w_ffn_down, g_final):
    B, S, D = x.shape
    assert (D, ctx.shape[1], S % KEY_CHUNK) == (D_MODEL, CTX_LEN, 0) and B <= 8
    cos_t, sin_t = _rope_tables(S)
    cs = jnp.zeros((16, D), F32).at[:B].set(c).at[8].set(c_ctx)
    mods = _ada(cs, w_ada, b_ada)
    ones_bd = _block_diag(jnp.ones((SGU_GROUPS, SGU_GROUP_DIM, SGU_GROUP_DIM), BF16))
    w_in16, w_out16 = w_in.astype(BF16), w_out.astype(BF16)
    w_gate16, w_up16, w_down16 = (w.astype(BF16) for w in (w_ffn_gate, w_ffn_up, w_ffn_down))
    stream = (ctx, x, 1)

    for l in range(DEPTH):
        last = l == DEPTH - 1
        lam_init = 0.8 - 0.6 * math.exp(-0.3 * l)
        mod = mods[l].reshape(16, N_MOD, 1, D)
        w_sp = jnp.transpose(w_spatial[l], (1, 0, 2)).reshape(CHUNK, SGU_GROUPS * CHUNK)
        b_sp = jnp.repeat(b_spatial[l].T, SGU_GROUP_DIM, axis=1)
        w_gate = jnp.stack([jnp.concatenate([_block_diag(w_rg_a[l, d]), _block_diag(w_rg_x[l, d])],
                                            axis=1) for d in range(2)]).astype(BF16)
        b_gate = jnp.concatenate([b_rg_a[l], b_rg_x[l]], axis=1)[:, None, :]

        qt, k, vt, lru_in, ys = _inproj(
            *stream, mod, g_norm1[l][None], l, w_in16, cos_t, sin_t, g_sgu[l][None],
            ones_bd, w_sp.astype(BF16), b_sp)
        t_off = 1 if last else 0
        ya = _attention(qt, k, vt, lam_q[l], lam_k[l], g_attn[l][None], lam_init, not last)
        yr = _lru(lru_in, conv_w[l], conv_b[l][None], w_gate, b_gate, lru_lambda[l][:, None, :])
        xs = _ffn(ya, yr, ys, *stream, mod, g_norm2[l][None], g_final[None], l, w_out16,
                  w_gate16, w_up16, w_down16, t_off, last)
        stream = (xs, xs, 0)
    return xs
```

```python
import functools
import math

import jax
import jax.numpy as jnp
from jax import lax
from jax.experimental import pallas as pl
from jax.experimental.pallas import tpu as pltpu

F32 = jnp.float32
BF16 = jnp.bfloat16

D_MODEL = 1024
DEPTH = 2
GRID_W = 64
CTX_LEN = 256
N_MOD = 6
EPS = 1e-6
ATTN_WIDTH = 512
LRU_WIDTH = 256
SGU_WIDTH = 256
ATTN_HEADS = 4
ATTN_HEADS_PER_STEP = 4
HEAD_DIM = 128
QK_DIM = 64
ROPE_AXIS_DIM = 32
ROPE_THETA = 10000.0
LRU_HEADS = 4
LRU_BLOCK = 64
CONV_WIDTH = 4
LRU_C = 8.0
SGU_GROUPS = 4
SGU_GROUP_DIM = 64
CHUNK = 128
FFN_HIDDEN = 2816
PROJ_WIDTH = 2560

TILE = 256
SUBTILES = 2
KEY_CHUNK = 4 * TILE
SUBLANES = 8
VMEM_LIMIT = 56 * 1024 * 1024


def _cparams(sem, flags=None):
    return pltpu.CompilerParams(dimension_semantics=sem, vmem_limit_bytes=VMEM_LIMIT, flags=flags)


def _const_spec(shape):
    nd = len(shape)
    return pl.BlockSpec(shape, lambda *_: (0,) * nd, pipeline_mode=pl.Buffered(1))


def _layer_spec(layer, shape):
    nd = len(shape)
    return pl.BlockSpec((None, *shape), lambda *_: (layer,) + (0,) * nd,
                        pipeline_mode=pl.Buffered(1))


def _sub_tile(i, j, t_off, n_tiles):
    return jnp.minimum(SUBTILES * i + j + t_off, n_tiles - 1)


def _token_spec(tile, width, first=0):
    return pl.BlockSpec((None, TILE, width), lambda b, i: (b, tile(i) - first, 0))


def _latent_spec(tile, lat_first):
    return pl.BlockSpec((None, TILE, D_MODEL),
                        lambda b, i: (b, jnp.maximum(tile(i), 1) - lat_first, 0))


def _mod_spec(tile):
    return pl.BlockSpec((None, N_MOD, 1, D_MODEL),
                        lambda b, i: (jnp.where(tile(i) == 0, 8, b), 0, 0, 0))


_CTX_SPEC = pl.BlockSpec((None, TILE, D_MODEL), lambda b, i: (b, 0, 0))


def _run_subtiles(body, t_off, n_tiles):
    rem = (n_tiles - t_off) % SUBTILES
    if rem == 0:
        body(SUBTILES)
    else:
        full = SUBTILES * (pl.program_id(1) + 1) + t_off <= n_tiles
        pl.when(full)(lambda: body(SUBTILES))
        pl.when(jnp.logical_not(full))(lambda: body(rem))


def _ada_kernel(c_ref, w_ref, b_ref, o_ref):
    c = c_ref[...]
    h = (c * jax.nn.sigmoid(c)).astype(BF16)
    o_ref[...] = jnp.dot(h, w_ref[...].astype(BF16), preferred_element_type=F32) + b_ref[...]


def _ada(cs, w_ada, b_ada):
    tn = 1536
    n_out = N_MOD * D_MODEL
    return pl.pallas_call(
        _ada_kernel,
        grid=(DEPTH, n_out // tn),
        in_specs=[
            pl.BlockSpec((16, D_MODEL), lambda l, n: (0, 0)),
            pl.BlockSpec((None, D_MODEL, tn), lambda l, n: (l, 0, n)),
            pl.BlockSpec((None, 1, tn), lambda l, n: (l, 0, n)),
        ],
        out_specs=pl.BlockSpec((None, 16, tn), lambda l, n: (l, 0, n)),
        out_shape=jax.ShapeDtypeStruct((DEPTH, 16, n_out), F32),
        compiler_params=_cparams(("arbitrary", "arbitrary")),
        name="ada_mod",
    )(cs, w_ada, b_ada.reshape(DEPTH, 1, n_out))


def _rope(x, cos, sin, first_half):
    partner = jnp.where(first_half, pltpu.roll(x, 128 - 16, 1), pltpu.roll(x, 16, 1))
    return x * cos + partner * sin


def _inproj_kernel(*refs, n_tiles):
    _run_subtiles(lambda n_sub: _inproj_tiles(refs, n_sub), 0, n_tiles)


def _inproj_tiles(refs, n_sub):
    per_sub = 4
    subs = [refs[per_sub * j:per_sub * (j + 1)] for j in range(n_sub)]
    (xc_ref, g1_ref, w_ref, gs_ref, ones_ref, wsp_ref, bsp_ref,
     qt_ref, k_ref, vt_ref, lru_ref, sgu_ref) = refs[per_sub * SUBTILES:]
    first_tile = SUBTILES * pl.program_id(1)

    p_all = []
    for j, (xl_ref, mod_ref, _, _) in enumerate(subs):
        x = jnp.where(first_tile + j == 0, xc_ref[...], xl_ref[...])
        rs = lax.rsqrt(jnp.mean(x * x, axis=-1, keepdims=True) + EPS)
        h = (x * rs * g1_ref[...]) * (1.0 + mod_ref[1]) + mod_ref[0]
        p_all.append(jnp.dot(h.astype(BF16), w_ref[...], preferred_element_type=F32))

    lane = lax.broadcasted_iota(jnp.int32, (TILE, HEAD_DIM), 1)
    first_half = (lane & 16) == 0
    grp = lax.broadcasted_iota(jnp.int32, (CHUNK, SGU_WIDTH), 1) // SGU_GROUP_DIM
    scale = QK_DIM ** -0.5 * math.log2(math.e)
    for j, (_, _, cos_ref, sin_ref) in enumerate(subs):
        p = p_all[j]
        rows = slice(j * TILE, (j + 1) * TILE)
        cos = cos_ref[...]
        sin = sin_ref[...]
        q_parts, k_parts = [], []
        for hd in range(ATTN_HEADS):
            q = p[:, hd * HEAD_DIM:(hd + 1) * HEAD_DIM]
            k = p[:, ATTN_WIDTH + hd * HEAD_DIM:ATTN_WIDTH + (hd + 1) * HEAD_DIM]
            q_parts.append(_rope(q, cos, sin, first_half) * scale)
            k_parts.append(_rope(k, cos, sin, first_half))
        q = jnp.concatenate(q_parts, axis=1)
        qt_ref[j] = q.T.astype(BF16)
        k_ref[rows, :] = jnp.concatenate(k_parts, axis=1).astype(BF16)
        vt_ref[j] = p[:, 2 * ATTN_WIDTH:3 * ATTN_WIDTH].T.astype(BF16)

        lx0 = 3 * ATTN_WIDTH
        lru_ref[rows, 0:LRU_WIDTH] = p[:, lx0:lx0 + LRU_WIDTH]
        lru_ref[rows, LRU_WIDTH:] = jax.nn.gelu(p[:, lx0 + LRU_WIDTH:lx0 + 2 * LRU_WIDTH])

        su0 = lx0 + 2 * LRU_WIDTH
        u = jax.nn.gelu(p[:, su0:su0 + SGU_WIDTH])
        v = jax.nn.gelu(p[:, su0 + SGU_WIDTH:su0 + 2 * SGU_WIDTH])
        ss = v * v
        hi = ss.astype(BF16)
        lo = (ss - hi.astype(F32)).astype(BF16)
        gsum = (jnp.dot(hi, ones_ref[...], preferred_element_type=F32)
                + jnp.dot(lo, ones_ref[...], preferred_element_type=F32))
        vn = v * lax.rsqrt(gsum * (1.0 / SGU_GROUP_DIM) + EPS) * gs_ref[...]
        for c in range(TILE // CHUNK):
            vc = vn[c * CHUNK:(c + 1) * CHUNK]
            rhs = jnp.concatenate(
                [jnp.where(grp == g, vc, 0.0) for g in range(SGU_GROUPS)], axis=0).astype(BF16)
            m = jnp.dot(wsp_ref[...], rhs, preferred_element_type=F32) + bsp_ref[...]
            r0 = j * TILE + c * CHUNK
            sgu_ref[r0:r0 + CHUNK, :] = (u[c * CHUNK:(c + 1) * CHUNK] * m).astype(BF16)


def _inproj(x_ctx, x_lat, lat_first, mod, g1, layer, w_in, cos_t, sin_t, g_sgu, ones_bd, w_sp,
            b_sp):
    B = x_lat.shape[0]
    T = cos_t.shape[0]
    nt = T // TILE

    def sub_specs(j):
        tile = lambda i: _sub_tile(i, j, 0, nt)
        table = pl.BlockSpec((TILE, HEAD_DIM), lambda b, i: (tile(i), 0))
        return [_latent_spec(tile, lat_first), _mod_spec(tile), table, table]

    tokens = []
    for j in range(SUBTILES):
        tokens += [x_lat, mod, cos_t, sin_t]
    rows = SUBTILES * TILE
    return pl.pallas_call(
        functools.partial(_inproj_kernel, n_tiles=nt),
        grid=(B, pl.cdiv(nt, SUBTILES)),
        in_specs=[
            *[spec for j in range(SUBTILES) for spec in sub_specs(j)],
            _CTX_SPEC,
            _const_spec((1, D_MODEL)),
            _layer_spec(layer, (D_MODEL, PROJ_WIDTH)),
            _const_spec((1, SGU_WIDTH)),
            _const_spec((SGU_WIDTH, SGU_WIDTH)),
            _const_spec((CHUNK, SGU_GROUPS * CHUNK)),
            _const_spec((CHUNK, SGU_WIDTH)),
        ],
        out_specs=[
            pl.BlockSpec((None, SUBTILES, ATTN_WIDTH, TILE), lambda b, i: (b, i, 0, 0)),
            pl.BlockSpec((None, rows, ATTN_WIDTH), lambda b, i: (b, i, 0)),
            pl.BlockSpec((None, SUBTILES, ATTN_WIDTH, TILE), lambda b, i: (b, i, 0, 0)),
            pl.BlockSpec((None, rows, 2 * LRU_WIDTH), lambda b, i: (b, i, 0)),
            pl.BlockSpec((None, rows, SGU_WIDTH), lambda b, i: (b, i, 0)),
        ],
        out_shape=[
            jax.ShapeDtypeStruct((B, nt, ATTN_WIDTH, TILE), BF16),
            jax.ShapeDtypeStruct((B, T, ATTN_WIDTH), BF16),
            jax.ShapeDtypeStruct((B, nt, ATTN_WIDTH, TILE), BF16),
            jax.ShapeDtypeStruct((B, T, 2 * LRU_WIDTH), F32),
            jax.ShapeDtypeStruct((B, T, SGU_WIDTH), BF16),
        ],
        compiler_params=_cparams(("parallel", "parallel")),
        name="inproj",
    )(*tokens, x_ctx, g1, w_in, g_sgu, ones_bd, w_sp, b_sp)


MAX_SCORE_GAP = 60.0

BF16_ROWS = 16
ATTN_Q_TILES = 1
ATTN_STREAMS = ATTN_Q_TILES * ATTN_HEADS_PER_STEP
QW = TILE
_ATTN_HEAD_SCRATCH = [
    pltpu.VMEM((HEAD_DIM + BF16_ROWS, 2 * QW), F32),
    pltpu.VMEM((1, 2 * QW), F32),
    pltpu.VMEM((SUBLANES, 2 * QW), F32),
    pltpu.VMEM((HEAD_DIM, 2 * QW), BF16)]


def _attn_kernel(*refs, lam_init, ctx_step):
    q_refs = refs[:ATTN_Q_TILES]
    k_ref, v_ref, lq_ref, lk_ref, g_ref, o_ref = refs[ATTN_Q_TILES:ATTN_Q_TILES + 6]
    scratch = refs[ATTN_Q_TILES + 6:]
    heads = range(ATTN_STREAMS)
    per_head = len(_ATTN_HEAD_SCRATCH)
    acc_ref, m_ref, smax_ref, rhs_ref = [scratch[n::per_head] for n in range(per_head)]
    hsl = [slice((h % ATTN_HEADS_PER_STEP) * HEAD_DIM, (h % ATTN_HEADS_PER_STEP + 1) * HEAD_DIM)
           for h in heads]
    qsl = [slice((h // ATTN_HEADS_PER_STEP) * TILE, (h // ATTN_HEADS_PER_STEP + 1) * TILE)
           for h in heads]
    row = lax.broadcasted_iota(jnp.int32, (HEAD_DIM, QW), 0)
    for h in heads:
        qf = q_refs[h // ATTN_HEADS_PER_STEP][hsl[h], :].astype(F32)
        rhs_ref[h][...] = jnp.concatenate([jnp.where(row < QK_DIM, qf, 0.0),
                                           jnp.where(row >= QK_DIM, qf, 0.0)],
                                          axis=1).astype(BF16)

    n_latent = (k_ref.shape[0] - CTX_LEN) // KEY_CHUNK

    def scores(h, c):
        if isinstance(c, int) and c == 0:
            rows = slice(0, CTX_LEN)
        else:
            rows = pl.ds(pl.multiple_of(CTX_LEN + (c - 1) * KEY_CHUNK, TILE), KEY_CHUNK)
        return jnp.dot(k_ref[rows, hsl[h]], rhs_ref[h][...], preferred_element_type=F32)

    def exponentiate(s, m):
        return jnp.exp2(s - m).astype(BF16)

    def accumulate(h, c, p):
        ctx = isinstance(c, int) and c == 0
        r = KEY_CHUNK // TILE
        v = (v_ref[0, hsl[h], :] if ctx else
             jnp.concatenate([v_ref[r * (c - 1) + 1 + t, hsl[h], :] for t in range(r)], axis=1))
        v = jnp.concatenate([v, jnp.ones((BF16_ROWS, v.shape[1]), BF16)], axis=0)
        upd = jnp.dot(v, p, preferred_element_type=F32)
        acc_ref[h][...] = upd if ctx else acc_ref[h][...] + upd

    def probabilities(h, c):
        s = scores(h, c)
        if c == 0:
            m_ref[h][...] = jnp.max(s, axis=0, keepdims=True)
        else:
            smax = jnp.max(s.reshape(KEY_CHUNK // SUBLANES, SUBLANES, 2 * QW), axis=0)
            smax_ref[h][...] = smax if c == 1 else jnp.maximum(smax_ref[h][...], smax)
        return exponentiate(s, m_ref[h][...])

    def stream(chunks):
        items = [(h, c) for c in chunks for h in heads]
        skew = 2
        ps = [probabilities(*item) for item in items[:skew]]
        for i, (h, c) in enumerate(items):
            if i + skew < len(items):
                ps.append(probabilities(*items[i + skew]))
            accumulate(h, c, ps[i])
            ps[i] = None

    def exact_softmax():
        for h in heads:
            m = jnp.maximum(m_ref[h][...], jnp.max(smax_ref[h][...], axis=0, keepdims=True))
            accumulate(h, 0, exponentiate(scores(h, 0), m))

            def chunk(c, carry, h=h, m=m):
                accumulate(h, c, exponentiate(scores(h, c), m))
                return carry

            lax.fori_loop(1, n_latent + 1, chunk, 0)

    def finish():
        t = jnp.sum(lq_ref[...] * lk_ref[...], axis=1, keepdims=True)
        e = jnp.exp(t)
        lam = e[0:1] - e[1:2] + lam_init
        for h in heads:
            acc = acc_ref[h][0:HEAD_DIM, :]
            inv = 1.0 / acc_ref[h][HEAD_DIM:HEAD_DIM + 1, :]
            ot = acc[:, :QW] * inv[:, :QW] - lam * (acc[:, QW:] * inv[:, QW:])
            rs = lax.rsqrt(jnp.mean(ot * ot, axis=0, keepdims=True) + EPS)
            y = (ot * rs).T * g_ref[...]
            o_ref[qsl[h], hsl[h]] = (y * (1.0 - lam_init)).astype(BF16)

    def context_queries():
        stream([0])
        finish()

    def latent_queries():
        stream(range(n_latent + 1))
        finish()
        gap = functools.reduce(
            jnp.maximum, [jnp.max(smax_ref[h][...] - m_ref[h][...]) for h in heads])

        @pl.when(gap > MAX_SCORE_GAP)
        def _():
            exact_softmax()
            finish()

    if ctx_step:
        pl.when(pl.program_id(2) == 0)(context_queries)
        pl.when(pl.program_id(2) != 0)(latent_queries)
    else:
        latent_queries()


def _attention(qt, k, vt, lam_q, lam_k, g_attn, lam_init, with_ctx):
    B, nt, _, _ = qt.shape
    T = k.shape[1]
    hp = ATTN_HEADS_PER_STEP
    ctx_step = 1 if with_ctx else 0
    steps = ctx_step + (nt - 1) // ATTN_Q_TILES
    assert (nt - 1) % ATTN_Q_TILES == 0
    kern = functools.partial(_attn_kernel, lam_init=lam_init, ctx_step=ctx_step)

    def q_spec(j):
        tile = lambda i: jnp.where(i < ctx_step, j, 1 + ATTN_Q_TILES * (i - ctx_step) + j)
        return pl.BlockSpec((None, None, hp * HEAD_DIM, TILE),
                            lambda b, h, i: (b, tile(i), h, 0))

    return pl.pallas_call(
        kern,
        grid=(B, ATTN_HEADS // hp, steps),
        in_specs=[
            *[q_spec(j) for j in range(ATTN_Q_TILES)],
            pl.BlockSpec((None, T, hp * HEAD_DIM), lambda b, h, i: (b, 0, h)),
            pl.BlockSpec((None, nt, hp * HEAD_DIM, TILE), lambda b, h, i: (b, 0, h, 0)),
            pl.BlockSpec((2, QK_DIM), lambda b, h, i: (0, 0)),
            pl.BlockSpec((2, QK_DIM), lambda b, h, i: (0, 0)),
            pl.BlockSpec((1, HEAD_DIM), lambda b, h, i: (0, 0)),
        ],
        out_specs=pl.BlockSpec((None, ATTN_Q_TILES * TILE, hp * HEAD_DIM),
                               lambda b, h, i: (b, i, h)),
        out_shape=jax.ShapeDtypeStruct((B, steps * ATTN_Q_TILES * TILE, ATTN_WIDTH), BF16),
        scratch_shapes=[shape for _ in range(ATTN_STREAMS) for shape in _ATTN_HEAD_SCRATCH],
        compiler_params=_cparams(("parallel", "parallel", "arbitrary")),
        name="diff_attn",
    )(*[qt] * ATTN_Q_TILES, k, vt, lam_q, lam_k, g_attn)


def _expm1(x, u):
    near = jnp.where(u == 1.0, x, (u - 1.0) * x / jnp.log(u))
    return jnp.where(jnp.abs(x) > 0.5, u - 1.0, near)


def _group_roll(x, shift):
    n, w = x.shape
    return pltpu.roll(x.reshape(n // SUBLANES, SUBLANES, w), shift % SUBLANES, 1).reshape(n, w)


def _chunk_scan(a, bt, carry, sub, reverse):
    n = a.shape[0]
    acc_a, acc_b = a, bt
    for k in (1, 2, 4):
        if reverse:
            valid = sub <= SUBLANES - 1 - k
            shift = n - k
        else:
            valid = sub >= k
            shift = k
        a_sh = jnp.where(valid, _group_roll(acc_a, shift), 1.0)
        b_sh = jnp.where(valid, _group_roll(acc_b, shift), 0.0)
        acc_b = acc_a * b_sh + acc_b
        acc_a = acc_a * a_sh
    groups = n // SUBLANES
    order = range(groups - 1, -1, -1) if reverse else range(groups)
    edge = 0 if reverse else SUBLANES - 1
    carries = [None] * groups
    c = carry
    for g in order:
        carries[g] = jnp.broadcast_to(c, (SUBLANES, a.shape[1]))
        r = g * SUBLANES + edge
        c = acc_a[r:r + 1] * c + acc_b[r:r + 1]
    h = acc_a * jnp.concatenate(carries, axis=0) + acc_b
    return h, c


def _lru_kernel(x_ref, cw_ref, cb_ref, wg_ref, bg_ref, lam_ref, o_ref, hs_ref, conv_ref):
    T = x_ref.shape[0]
    nt = T // TILE
    row = lax.broadcasted_iota(jnp.int32, (TILE, LRU_WIDTH), 0)
    sub = row & (SUBLANES - 1)
    cw = cw_ref[...]
    cb = cb_ref[...]

    def conv(j):
        r0 = pl.multiple_of(j * TILE, TILE)
        cur = x_ref[pl.ds(r0, TILE), 0:LRU_WIDTH]
        pr = pl.multiple_of(jnp.maximum(r0 - SUBLANES, 0), SUBLANES)
        nx = pl.multiple_of(jnp.minimum(r0 + TILE, T - SUBLANES), SUBLANES)
        prev8 = x_ref[pl.ds(pr, SUBLANES), 0:LRU_WIDTH]
        next8 = x_ref[pl.ds(nx, SUBLANES), 0:LRU_WIDTH]
        has_prev = j >= 2
        has_next = jnp.logical_and(j >= 1, j <= nt - 2)
        p1 = jnp.where(has_prev, prev8[SUBLANES - 1:SUBLANES], 0.0)
        n0 = jnp.where(has_next, next8[0:1], 0.0)
        n1 = jnp.where(has_next, next8[1:2], 0.0)
        xm1 = jnp.where(row == 0, p1, pltpu.roll(cur, 1, 0))
        xp1 = jnp.where(row == TILE - 1, n0, pltpu.roll(cur, TILE - 1, 0))
        xp2 = jnp.where(row == TILE - 2, n0,
                        jnp.where(row == TILE - 1, n1, pltpu.roll(cur, TILE - 2, 0)))
        return cw[0:1] * xm1 + cw[1:2] * cur + cw[2:3] * xp1 + cw[3:4] * xp2 + cb

    def direction(j, d, carry):
        rows = pl.ds(pl.multiple_of(j * TILE, TILE), TILE)
        if d == 0:
            c = conv(j)
            conv_ref[rows, :] = c
        else:
            c = conv_ref[rows, :]
        gates = jnp.dot(c.astype(BF16), wg_ref[d], preferred_element_type=F32) + bg_ref[d]
        r = jax.nn.sigmoid(gates[:, :LRU_WIDTH])
        i = jax.nn.sigmoid(gates[:, LRU_WIDTH:])
        log_a = (-LRU_C * r) * jax.nn.softplus(-lam_ref[d])
        a = jnp.exp(log_a)
        y = -_expm1(2.0 * log_a, a * a)
        root = jnp.where(y > 0.0, y * lax.rsqrt(y), 0.0)
        bt = root * (i * c)
        return _chunk_scan(a, bt, carry, sub, reverse=(d == 1))

    def fwd(j, carry):
        h, carry = direction(j, 0, carry)
        hs_ref[pl.ds(pl.multiple_of(j * TILE, TILE), TILE), :] = h
        return carry

    def rev(n, carry):
        j = jnp.where(n == 0, 0, nt - n)
        h, carry = direction(j, 1, carry)
        r0 = pl.multiple_of(j * TILE, TILE)
        gate = x_ref[pl.ds(r0, TILE), LRU_WIDTH:]
        o_ref[pl.ds(r0, TILE), :] = ((hs_ref[pl.ds(r0, TILE), :] + h) * gate).astype(BF16)
        return carry

    zero = jnp.zeros((1, LRU_WIDTH), F32)
    lax.fori_loop(0, nt, fwd, zero)
    lax.fori_loop(0, nt, rev, zero)


def _lru(lru_in, conv_w, conv_b, w_gate, b_gate, lam):
    B, T, _ = lru_in.shape
    return pl.pallas_call(
        _lru_kernel,
        grid=(B,),
        in_specs=[
            pl.BlockSpec((None, T, 2 * LRU_WIDTH), lambda b: (b, 0, 0)),
            _const_spec((CONV_WIDTH, LRU_WIDTH)),
            _const_spec((1, LRU_WIDTH)),
            _const_spec((2, LRU_WIDTH, 2 * LRU_WIDTH)),
            _const_spec((2, 1, 2 * LRU_WIDTH)),
            _const_spec((2, 1, LRU_WIDTH)),
        ],
        out_specs=pl.BlockSpec((None, T, LRU_WIDTH), lambda b: (b, 0, 0)),
        out_shape=jax.ShapeDtypeStruct((B, T, LRU_WIDTH), BF16),
        scratch_shapes=[pltpu.VMEM((T, LRU_WIDTH), F32),
                        pltpu.VMEM((T, LRU_WIDTH), F32)],
        compiler_params=_cparams(("parallel",)),
        name="rglru",
    )(lru_in, conv_w, conv_b, w_gate, b_gate, lam)


def _ffn_kernel(*refs, t_off, n_tiles, final):
    _run_subtiles(lambda n_sub: _ffn_tiles(refs, n_sub, t_off, final), t_off, n_tiles)


def _ffn_tiles(refs, n_sub, t_off, final):
    per_sub = 5
    subs = [refs[per_sub * j:per_sub * (j + 1)] for j in range(n_sub)]
    xc_ref, g2_ref, gf_ref, wo_ref, wg_ref, wu_ref, wd_ref, o_ref = refs[per_sub * SUBTILES:]
    first_tile = SUBTILES * pl.program_id(1) + t_off

    proj = [jnp.dot(jnp.concatenate([ya[...], yr[...], ys[...]], axis=-1), wo_ref[...],
                    preferred_element_type=F32) for ya, yr, ys, _, _ in subs]
    x1, g, u = [], [], []
    for j, (_, _, _, xl_ref, mod_ref) in enumerate(subs):
        x = jnp.where(first_tile + j == 0, xc_ref[...], xl_ref[...])
        x1.append(x + mod_ref[2] * proj[j])
        rs = lax.rsqrt(jnp.mean(x1[j] * x1[j], axis=-1, keepdims=True) + EPS)
        h = ((x1[j] * rs * g2_ref[...]) * (1.0 + mod_ref[4]) + mod_ref[3]).astype(BF16)
        g.append(jnp.dot(h, wg_ref[...], preferred_element_type=F32))
        u.append(jnp.dot(h, wu_ref[...], preferred_element_type=F32))
    for j, (_, _, _, _, mod_ref) in enumerate(subs):
        act = ((g[j] * jax.nn.sigmoid(g[j])) * u[j]).astype(BF16)
        x2 = x1[j] + mod_ref[5] * jnp.dot(act, wd_ref[...], preferred_element_type=F32)
        if final:
            x2 = x2 * lax.rsqrt(jnp.mean(x2 * x2, axis=-1, keepdims=True) + EPS) * gf_ref[...]
        o_ref[j * TILE:(j + 1) * TILE, :] = x2


def _ffn(ya, yr, ys, x_ctx, x_lat, lat_first, mod, g2, g_final, layer, w_out, w_gate, w_up,
         w_down, t_off, final):
    B, T, _ = yr.shape
    n_tiles = T // TILE
    nt = n_tiles - t_off
    with_ctx = t_off == 0
    ya_block = ((lambda t: jnp.where(t == 0, 0, ATTN_Q_TILES + t - 1)) if with_ctx
                else (lambda t: t - 1))
    kern = functools.partial(_ffn_kernel, t_off=t_off, n_tiles=n_tiles, final=final)

    def sub_specs(j):
        tile = lambda i: _sub_tile(i, j, t_off, n_tiles)
        return [pl.BlockSpec((None, TILE, ATTN_WIDTH), lambda b, i: (b, ya_block(tile(i)), 0)),
                _token_spec(tile, LRU_WIDTH),
                _token_spec(tile, SGU_WIDTH), _latent_spec(tile, lat_first), _mod_spec(tile)]

    tokens = []
    for j in range(SUBTILES):
        tokens += [ya, yr, ys, x_lat, mod]
    return pl.pallas_call(
        kern,
        grid=(B, pl.cdiv(nt, SUBTILES)),
        in_specs=[
            *[spec for j in range(SUBTILES) for spec in sub_specs(j)],
            _CTX_SPEC,
            _const_spec((1, D_MODEL)),
            _const_spec((1, D_MODEL)),
            _layer_spec(layer, (D_MODEL, D_MODEL)),
            _layer_spec(layer, (D_MODEL, FFN_HIDDEN)),
            _layer_spec(layer, (D_MODEL, FFN_HIDDEN)),
            _layer_spec(layer, (FFN_HIDDEN, D_MODEL)),
        ],
        out_specs=pl.BlockSpec((None, SUBTILES * TILE, D_MODEL), lambda b, i: (b, i, 0)),
        out_shape=jax.ShapeDtypeStruct((B, nt * TILE, D_MODEL), F32),
        compiler_params=_cparams(("parallel", "parallel")),
        name="outproj_ffn",
    )(*tokens, x_ctx, g2, g_final, w_out, w_gate, w_up, w_down)


def _rope_tables(seq):
    rows = seq // GRID_W
    row = jnp.repeat(jnp.arange(rows), GRID_W).astype(F32)
    col = jnp.tile(jnp.arange(GRID_W), rows).astype(F32)
    n = ROPE_AXIS_DIM // 2
    inv = ROPE_THETA ** (-jnp.arange(n, dtype=F32) / n)
    ang_r = row[:, None] * inv[None, :]
    ang_c = col[:, None] * inv[None, :]
    cr, sr, cc, sc = jnp.cos(ang_r), jnp.sin(ang_r), jnp.cos(ang_c), jnp.sin(ang_c)
    cos = jnp.tile(jnp.concatenate([cr, cr, cc, cc], axis=1), (1, 2))
    sin = jnp.tile(jnp.concatenate([-sr, sr, -sc, sc], axis=1), (1, 2))
    cos = jnp.concatenate([jnp.ones((CTX_LEN, HEAD_DIM), F32), cos], axis=0)
    sin = jnp.concatenate([jnp.zeros((CTX_LEN, HEAD_DIM), F32), sin], axis=0)
    return cos, sin


def _block_diag(w):
    eye = jnp.eye(w.shape[0], dtype=w.dtype)
    return jnp.einsum('hij,hk->hikj', w, eye).reshape(w.shape[0] * w.shape[1],
                                                      w.shape[0] * w.shape[2])


def kernel(x, c, ctx, c_ctx, w_ada, b_ada, g_norm1, g_norm2, w_in, lam_q, lam_k, g_attn, conv_w,
           conv_b, w_rg_a, b_rg_a, w_rg_x, b_rg_x, lru_lambda, g_sgu, w_spatial, b_spatial, w_out,
           w_ffn_gate, w_ffn_up, w_ffn_down, g_final):
    B, S, D = x.shape
    assert (D, ctx.shape[1], S % KEY_CHUNK) == (D_MODEL, CTX_LEN, 0) and B <= 8
    cos_t, sin_t = _rope_tables(S)
    cs = jnp.zeros((16, D), F32).at[:B].set(c).at[8].set(c_ctx)
    mods = _ada(cs, w_ada, b_ada)
    ones_bd = _block_diag(jnp.ones((SGU_GROUPS, SGU_GROUP_DIM, SGU_GROUP_DIM), BF16))
    w_in16, w_out16 = w_in.astype(BF16), w_out.astype(BF16)
    w_gate16, w_up16, w_down16 = (w.astype(BF16) for w in (w_ffn_gate, w_ffn_up, w_ffn_down))
    stream = (ctx, x, 1)

    for l in range(DEPTH):
        last = l == DEPTH - 1
        lam_init = 0.8 - 0.6 * math.exp(-0.3 * l)
        mod = mods[l].reshape(16, N_MOD, 1, D)
        w_sp = jnp.transpose(w_spatial[l], (1, 0, 2)).reshape(CHUNK, SGU_GROUPS * CHUNK)
        b_sp = jnp.repeat(b_spatial[l].T, SGU_GROUP_DIM, axis=1)
        w_gate = jnp.stack([jnp.concatenate([_block_diag(w_rg_a[l, d]), _block_diag(w_rg_x[l, d])],
                                            axis=1) for d in range(2)]).astype(BF16)
        b_gate = jnp.concatenate([b_rg_a[l], b_rg_x[l]], axis=1)[:, None, :]

        qt, k, vt, lru_in, ys = _inproj(
            *stream, mod, g_norm1[l][None], l, w_in16, cos_t, sin_t, g_sgu[l][None],
            ones_bd, w_sp.astype(BF16), b_sp)
        t_off = 1 if last else 0
        ya = _attention(qt, k, vt, lam_q[l], lam_k[l], g_attn[l][None], lam_init, not last)
        yr = _lru(lru_in, conv_w[l], conv_b[l][None], w_gate, b_gate, lru_lambda[l][:, None, :])
        xs = _ffn(ya, yr, ys, *stream, mod, g_norm2[l][None], g_final[None], l, w_out16,
                  w_gate16, w_up16, w_down16, t_off, last)
        stream = (xs, xs, 0)
    return xs
```

```python
import functools
import math

import jax
import jax.numpy as jnp
from jax import lax
from jax.experimental import pallas as pl
from jax.experimental.pallas import tpu as pltpu

F32 = jnp.float32
BF16 = jnp.bfloat16

D_MODEL = 1024
DEPTH = 2
GRID_W = 64
CTX_LEN = 256
N_MOD = 6
EPS = 1e-6
ATTN_WIDTH = 512
LRU_WIDTH = 256
SGU_WIDTH = 256
ATTN_HEADS = 4
ATTN_HEADS_PER_STEP = 4
HEAD_DIM = 128
QK_DIM = 64
ROPE_AXIS_DIM = 32
ROPE_THETA = 10000.0
LRU_HEADS = 4
LRU_BLOCK = 64
CONV_WIDTH = 4
LRU_C = 8.0
SGU_GROUPS = 4
SGU_GROUP_DIM = 64
CHUNK = 128
FFN_HIDDEN = 2816
PROJ_WIDTH = 2560

TILE = 256
SUBTILES = 2
KEY_CHUNK = 2 * TILE
SUBLANES = 8
VMEM_LIMIT = 56 * 1024 * 1024


def _cparams(sem, flags=None):
    return pltpu.CompilerParams(dimension_semantics=sem, vmem_limit_bytes=VMEM_LIMIT, flags=flags)


def _const_spec(shape):
    nd = len(shape)
    return pl.BlockSpec(shape, lambda *_: (0,) * nd, pipeline_mode=pl.Buffered(1))


def _layer_spec(layer, shape):
    nd = len(shape)
    return pl.BlockSpec((None, *shape), lambda *_: (layer,) + (0,) * nd,
                        pipeline_mode=pl.Buffered(1))


def _sub_tile(i, j, t_off, n_tiles):
    return jnp.minimum(SUBTILES * i + j + t_off, n_tiles - 1)


def _token_spec(tile, width, first=0):
    return pl.BlockSpec((None, TILE, width), lambda b, i: (b, tile(i) - first, 0))


def _latent_spec(tile, lat_first):
    return pl.BlockSpec((None, TILE, D_MODEL),
                        lambda b, i: (b, jnp.maximum(tile(i), 1) - lat_first, 0))


def _mod_spec(tile):
    return pl.BlockSpec((None, N_MOD, 1, D_MODEL),
                        lambda b, i: (jnp.where(tile(i) == 0, 8, b), 0, 0, 0))


_CTX_SPEC = pl.BlockSpec((None, TILE, D_MODEL), lambda b, i: (b, 0, 0))


def _run_subtiles(body, t_off, n_tiles):
    rem = (n_tiles - t_off) % SUBTILES
    if rem == 0:
        body(SUBTILES)
    else:
        full = SUBTILES * (pl.program_id(1) + 1) + t_off <= n_tiles
        pl.when(full)(lambda: body(SUBTILES))
        pl.when(jnp.logical_not(full))(lambda: body(rem))


def _ada_kernel(c_ref, w_ref, b_ref, o_ref):
    c = c_ref[...]
    h = (c * jax.nn.sigmoid(c)).astype(BF16)
    o_ref[...] = jnp.dot(h, w_ref[...].astype(BF16), preferred_element_type=F32) + b_ref[...]


def _ada(cs, w_ada, b_ada):
    tn = 1536
    n_out = N_MOD * D_MODEL
    return pl.pallas_call(
        _ada_kernel,
        grid=(DEPTH, n_out // tn),
        in_specs=[
            pl.BlockSpec((16, D_MODEL), lambda l, n: (0, 0)),
            pl.BlockSpec((None, D_MODEL, tn), lambda l, n: (l, 0, n)),
            pl.BlockSpec((None, 1, tn), lambda l, n: (l, 0, n)),
        ],
        out_specs=pl.BlockSpec((None, 16, tn), lambda l, n: (l, 0, n)),
        out_shape=jax.ShapeDtypeStruct((DEPTH, 16, n_out), F32),
        compiler_params=_cparams(("arbitrary", "arbitrary")),
        name="ada_mod",
    )(cs, w_ada, b_ada.reshape(DEPTH, 1, n_out))


def _rope(x, cos, sin, first_half):
    partner = jnp.where(first_half, pltpu.roll(x, 128 - 16, 1), pltpu.roll(x, 16, 1))
    return x * cos + partner * sin


def _inproj_kernel(*refs, n_tiles):
    _run_subtiles(lambda n_sub: _inproj_tiles(refs, n_sub), 0, n_tiles)


def _inproj_tiles(refs, n_sub):
    per_sub = 4
    subs = [refs[per_sub * j:per_sub * (j + 1)] for j in range(n_sub)]
    (xc_ref, g1_ref, w_ref, gs_ref, ones_ref, wsp_ref, bsp_ref,
     qt_ref, k_ref, vt_ref, lru_ref, sgu_ref) = refs[per_sub * SUBTILES:]
    first_tile = SUBTILES * pl.program_id(1)

    p_all = []
    for j, (xl_ref, mod_ref, _, _) in enumerate(subs):
        x = jnp.where(first_tile + j == 0, xc_ref[...], xl_ref[...])
        rs = lax.rsqrt(jnp.mean(x * x, axis=-1, keepdims=True) + EPS)
        h = (x * rs * g1_ref[...]) * (1.0 + mod_ref[1]) + mod_ref[0]
        p_all.append(jnp.dot(h.astype(BF16), w_ref[...], preferred_element_type=F32))

    lane = lax.broadcasted_iota(jnp.int32, (TILE, HEAD_DIM), 1)
    first_half = (lane & 16) == 0
    grp = lax.broadcasted_iota(jnp.int32, (CHUNK, SGU_WIDTH), 1) // SGU_GROUP_DIM
    scale = QK_DIM ** -0.5 * math.log2(math.e)
    for j, (_, _, cos_ref, sin_ref) in enumerate(subs):
        p = p_all[j]
        rows = slice(j * TILE, (j + 1) * TILE)
        cos = cos_ref[...]
        sin = sin_ref[...]
        q_parts, k_parts = [], []
        for hd in range(ATTN_HEADS):
            q = p[:, hd * HEAD_DIM:(hd + 1) * HEAD_DIM]
            k = p[:, ATTN_WIDTH + hd * HEAD_DIM:ATTN_WIDTH + (hd + 1) * HEAD_DIM]
            q_parts.append(_rope(q, cos, sin, first_half) * scale)
            k_parts.append(_rope(k, cos, sin, first_half))
        q = jnp.concatenate(q_parts, axis=1)
        qt_ref[j] = q.T.astype(BF16)
        k_ref[rows, :] = jnp.concatenate(k_parts, axis=1).astype(BF16)
        vt_ref[j] = p[:, 2 * ATTN_WIDTH:3 * ATTN_WIDTH].T.astype(BF16)

        lx0 = 3 * ATTN_WIDTH
        lru_ref[rows, 0:LRU_WIDTH] = p[:, lx0:lx0 + LRU_WIDTH]
        lru_ref[rows, LRU_WIDTH:] = jax.nn.gelu(p[:, lx0 + LRU_WIDTH:lx0 + 2 * LRU_WIDTH])

        su0 = lx0 + 2 * LRU_WIDTH
        u = jax.nn.gelu(p[:, su0:su0 + SGU_WIDTH])
        v = jax.nn.gelu(p[:, su0 + SGU_WIDTH:su0 + 2 * SGU_WIDTH])
        ss = v * v
        hi = ss.astype(BF16)
        lo = (ss - hi.astype(F32)).astype(BF16)
        gsum = (jnp.dot(hi, ones_ref[...], preferred_element_type=F32)
                + jnp.dot(lo, ones_ref[...], preferred_element_type=F32))
        vn = v * lax.rsqrt(gsum * (1.0 / SGU_GROUP_DIM) + EPS) * gs_ref[...]
        for c in range(TILE // CHUNK):
            vc = vn[c * CHUNK:(c + 1) * CHUNK]
            rhs = jnp.concatenate(
                [jnp.where(grp == g, vc, 0.0) for g in range(SGU_GROUPS)], axis=0).astype(BF16)
            m = jnp.dot(wsp_ref[...], rhs, preferred_element_type=F32) + bsp_ref[...]
            r0 = j * TILE + c * CHUNK
            sgu_ref[r0:r0 + CHUNK, :] = (u[c * CHUNK:(c + 1) * CHUNK] * m).astype(BF16)


def _inproj(x_ctx, x_lat, lat_first, mod, g1, layer, w_in, cos_t, sin_t, g_sgu, ones_bd, w_sp,
            b_sp):
    B = x_lat.shape[0]
    T = cos_t.shape[0]
    nt = T // TILE

    def sub_specs(j):
        tile = lambda i: _sub_tile(i, j, 0, nt)
        table = pl.BlockSpec((TILE, HEAD_DIM), lambda b, i: (tile(i), 0))
        return [_latent_spec(tile, lat_first), _mod_spec(tile), table, table]

    tokens = []
    for j in range(SUBTILES):
        tokens += [x_lat, mod, cos_t, sin_t]
    rows = SUBTILES * TILE
    return pl.pallas_call(
        functools.partial(_inproj_kernel, n_tiles=nt),
        grid=(B, pl.cdiv(nt, SUBTILES)),
        in_specs=[
            *[spec for j in range(SUBTILES) for spec in sub_specs(j)],
            _CTX_SPEC,
            _const_spec((1, D_MODEL)),
            _layer_spec(layer, (D_MODEL, PROJ_WIDTH)),
            _const_spec((1, SGU_WIDTH)),
            _const_spec((SGU_WIDTH, SGU_WIDTH)),
            _const_spec((CHUNK, SGU_GROUPS * CHUNK)),
            _const_spec((CHUNK, SGU_WIDTH)),
        ],
        out_specs=[
            pl.BlockSpec((None, SUBTILES, ATTN_WIDTH, TILE), lambda b, i: (b, i, 0, 0)),
            pl.BlockSpec((None, rows, ATTN_WIDTH), lambda b, i: (b, i, 0)),
            pl.BlockSpec((None, SUBTILES, ATTN_WIDTH, TILE), lambda b, i: (b, i, 0, 0)),
            pl.BlockSpec((None, rows, 2 * LRU_WIDTH), lambda b, i: (b, i, 0)),
            pl.BlockSpec((None, rows, SGU_WIDTH), lambda b, i: (b, i, 0)),
        ],
        out_shape=[
            jax.ShapeDtypeStruct((B, nt, ATTN_WIDTH, TILE), BF16),
            jax.ShapeDtypeStruct((B, T, ATTN_WIDTH), BF16),
            jax.ShapeDtypeStruct((B, nt, ATTN_WIDTH, TILE), BF16),
            jax.ShapeDtypeStruct((B, T, 2 * LRU_WIDTH), F32),
            jax.ShapeDtypeStruct((B, T, SGU_WIDTH), BF16),
        ],
        compiler_params=_cparams(("parallel", "parallel")),
        name="inproj",
    )(*tokens, x_ctx, g1, w_in, g_sgu, ones_bd, w_sp, b_sp)


MAX_SCORE_GAP = 60.0

BF16_ROWS = 16
ATTN_Q_TILES = 1
QW = ATTN_Q_TILES * TILE
_ATTN_HEAD_SCRATCH = [
    pltpu.VMEM((HEAD_DIM + BF16_ROWS, 2 * QW), F32),
    pltpu.VMEM((1, 2 * QW), F32),
    pltpu.VMEM((SUBLANES, 2 * QW), F32),
    pltpu.VMEM((HEAD_DIM, 2 * QW), BF16)]


def _attn_kernel(*refs, lam_init, ctx_step):
    q_refs = refs[:ATTN_Q_TILES]
    k_ref, v_ref, lq_ref, lk_ref, g_ref, o_ref = refs[ATTN_Q_TILES:ATTN_Q_TILES + 6]
    scratch = refs[ATTN_Q_TILES + 6:]
    heads = range(ATTN_HEADS_PER_STEP)
    per_head = len(_ATTN_HEAD_SCRATCH)
    acc_ref, m_ref, smax_ref, rhs_ref = [scratch[n::per_head] for n in range(per_head)]
    hsl = [slice(h * HEAD_DIM, (h + 1) * HEAD_DIM) for h in heads]
    row = lax.broadcasted_iota(jnp.int32, (HEAD_DIM, QW), 0)
    for h in heads:
        qf = jnp.concatenate([q[hsl[h], :] for q in q_refs], axis=1).astype(F32)
        rhs_ref[h][...] = jnp.concatenate([jnp.where(row < QK_DIM, qf, 0.0),
                                           jnp.where(row >= QK_DIM, qf, 0.0)],
                                          axis=1).astype(BF16)

    n_latent = (k_ref.shape[0] - CTX_LEN) // KEY_CHUNK

    def scores(h, c):
        if isinstance(c, int) and c == 0:
            rows = slice(0, CTX_LEN)
        else:
            rows = pl.ds(pl.multiple_of(CTX_LEN + (c - 1) * KEY_CHUNK, TILE), KEY_CHUNK)
        return jnp.dot(k_ref[rows, hsl[h]], rhs_ref[h][...], preferred_element_type=F32)

    def exponentiate(s, m):
        return jnp.exp2(s - m).astype(BF16)

    def accumulate(h, c, p):
        ctx = isinstance(c, int) and c == 0
        v = (v_ref[0, hsl[h], :] if ctx else
             jnp.concatenate([v_ref[2 * c - 1, hsl[h], :], v_ref[2 * c, hsl[h], :]], axis=1))
        v = jnp.concatenate([v, jnp.ones((BF16_ROWS, v.shape[1]), BF16)], axis=0)
        upd = jnp.dot(v, p, preferred_element_type=F32)
        acc_ref[h][...] = upd if ctx else acc_ref[h][...] + upd

    def probabilities(h, c):
        s = scores(h, c)
        if c == 0:
            m_ref[h][...] = jnp.max(s, axis=0, keepdims=True)
        else:
            smax = jnp.max(s.reshape(KEY_CHUNK // SUBLANES, SUBLANES, 2 * QW), axis=0)
            smax_ref[h][...] = smax if c == 1 else jnp.maximum(smax_ref[h][...], smax)
        return exponentiate(s, m_ref[h][...])

    def stream(chunks):
        items = [(h, c) for c in chunks for h in heads]
        skew = 4
        ps = [probabilities(*item) for item in items[:skew]]
        for i, (h, c) in enumerate(items):
            if i + skew < len(items):
                ps.append(probabilities(*items[i + skew]))
            accumulate(h, c, ps[i])
            ps[i] = None

    def exact_softmax():
        for h in heads:
            m = jnp.maximum(m_ref[h][...], jnp.max(smax_ref[h][...], axis=0, keepdims=True))
            accumulate(h, 0, exponentiate(scores(h, 0), m))

            def chunk(c, carry, h=h, m=m):
                accumulate(h, c, exponentiate(scores(h, c), m))
                return carry

            lax.fori_loop(1, n_latent + 1, chunk, 0)

    def finish():
        t = jnp.sum(lq_ref[...] * lk_ref[...], axis=1, keepdims=True)
        e = jnp.exp(t)
        lam = e[0:1] - e[1:2] + lam_init
        for h in heads:
            acc = acc_ref[h][0:HEAD_DIM, :]
            inv = 1.0 / acc_ref[h][HEAD_DIM:HEAD_DIM + 1, :]
            ot = acc[:, :QW] * inv[:, :QW] - lam * (acc[:, QW:] * inv[:, QW:])
            rs = lax.rsqrt(jnp.mean(ot * ot, axis=0, keepdims=True) + EPS)
            y = (ot * rs).T * g_ref[...]
            o_ref[:, hsl[h]] = (y * (1.0 - lam_init)).astype(BF16)

    def context_queries():
        stream([0])
        finish()

    def latent_queries():
        stream(range(n_latent + 1))
        finish()
        gap = functools.reduce(
            jnp.maximum, [jnp.max(smax_ref[h][...] - m_ref[h][...]) for h in heads])

        @pl.when(gap > MAX_SCORE_GAP)
        def _():
            exact_softmax()
            finish()

    if ctx_step:
        pl.when(pl.program_id(2) == 0)(context_queries)
        pl.when(pl.program_id(2) != 0)(latent_queries)
    else:
        latent_queries()


def _attention(qt, k, vt, lam_q, lam_k, g_attn, lam_init, with_ctx):
    B, nt, _, _ = qt.shape
    T = k.shape[1]
    hp = ATTN_HEADS_PER_STEP
    ctx_step = 1 if with_ctx else 0
    steps = ctx_step + (nt - 1) // ATTN_Q_TILES
    assert (nt - 1) % ATTN_Q_TILES == 0
    kern = functools.partial(_attn_kernel, lam_init=lam_init, ctx_step=ctx_step)

    def q_spec(j):
        tile = lambda i: jnp.where(i < ctx_step, j, 1 + ATTN_Q_TILES * (i - ctx_step) + j)
        return pl.BlockSpec((None, None, hp * HEAD_DIM, TILE),
                            lambda b, h, i: (b, tile(i), h, 0))

    return pl.pallas_call(
        kern,
        grid=(B, ATTN_HEADS // hp, steps),
        in_specs=[
            *[q_spec(j) for j in range(ATTN_Q_TILES)],
            pl.BlockSpec((None, T, hp * HEAD_DIM), lambda b, h, i: (b, 0, h)),
            pl.BlockSpec((None, nt, hp * HEAD_DIM, TILE), lambda b, h, i: (b, 0, h, 0)),
            pl.BlockSpec((2, QK_DIM), lambda b, h, i: (0, 0)),
            pl.BlockSpec((2, QK_DIM), lambda b, h, i: (0, 0)),
            pl.BlockSpec((1, HEAD_DIM), lambda b, h, i: (0, 0)),
        ],
        out_specs=pl.BlockSpec((None, QW, hp * HEAD_DIM), lambda b, h, i: (b, i, h)),
        out_shape=jax.ShapeDtypeStruct((B, steps * QW, ATTN_WIDTH), BF16),
        scratch_shapes=[shape for _ in range(hp) for shape in _ATTN_HEAD_SCRATCH],
        compiler_params=_cparams(("parallel", "parallel", "arbitrary")),
        name="diff_attn",
    )(*[qt] * ATTN_Q_TILES, k, vt, lam_q, lam_k, g_attn)


def _expm1(x, u):
    near = jnp.where(u == 1.0, x, (u - 1.0) * x / jnp.log(u))
    return jnp.where(jnp.abs(x) > 0.5, u - 1.0, near)


def _group_roll(x, shift):
    n, w = x.shape
    return pltpu.roll(x.reshape(n // SUBLANES, SUBLANES, w), shift % SUBLANES, 1).reshape(n, w)


def _chunk_scan(a, bt, carry, sub, reverse):
    n = a.shape[0]
    acc_a, acc_b = a, bt
    for k in (1, 2, 4):
        if reverse:
            valid = sub <= SUBLANES - 1 - k
            shift = n - k
        else:
            valid = sub >= k
            shift = k
        a_sh = jnp.where(valid, _group_roll(acc_a, shift), 1.0)
        b_sh = jnp.where(valid, _group_roll(acc_b, shift), 0.0)
        acc_b = acc_a * b_sh + acc_b
        acc_a = acc_a * a_sh
    groups = n // SUBLANES
    order = range(groups - 1, -1, -1) if reverse else range(groups)
    edge = 0 if reverse else SUBLANES - 1
    carries = [None] * groups
    c = carry
    for g in order:
        carries[g] = jnp.broadcast_to(c, (SUBLANES, a.shape[1]))
        r = g * SUBLANES + edge
        c = acc_a[r:r + 1] * c + acc_b[r:r + 1]
    h = acc_a * jnp.concatenate(carries, axis=0) + acc_b
    return h, c


def _lru_kernel(x_ref, cw_ref, cb_ref, wg_ref, bg_ref, lam_ref, o_ref, hs_ref, conv_ref):
    T = x_ref.shape[0]
    nt = T // TILE
    row = lax.broadcasted_iota(jnp.int32, (TILE, LRU_WIDTH), 0)
    sub = row & (SUBLANES - 1)
    cw = cw_ref[...]
    cb = cb_ref[...]

    def conv(j):
        r0 = pl.multiple_of(j * TILE, TILE)
        cur = x_ref[pl.ds(r0, TILE), 0:LRU_WIDTH]
        pr = pl.multiple_of(jnp.maximum(r0 - SUBLANES, 0), SUBLANES)
        nx = pl.multiple_of(jnp.minimum(r0 + TILE, T - SUBLANES), SUBLANES)
        prev8 = x_ref[pl.ds(pr, SUBLANES), 0:LRU_WIDTH]
        next8 = x_ref[pl.ds(nx, SUBLANES), 0:LRU_WIDTH]
        has_prev = j >= 2
        has_next = jnp.logical_and(j >= 1, j <= nt - 2)
        p1 = jnp.where(has_prev, prev8[SUBLANES - 1:SUBLANES], 0.0)
        n0 = jnp.where(has_next, next8[0:1], 0.0)
        n1 = jnp.where(has_next, next8[1:2], 0.0)
        xm1 = jnp.where(row == 0, p1, pltpu.roll(cur, 1, 0))
        xp1 = jnp.where(row == TILE - 1, n0, pltpu.roll(cur, TILE - 1, 0))
        xp2 = jnp.where(row == TILE - 2, n0,
                        jnp.where(row == TILE - 1, n1, pltpu.roll(cur, TILE - 2, 0)))
        return cw[0:1] * xm1 + cw[1:2] * cur + cw[2:3] * xp1 + cw[3:4] * xp2 + cb

    def direction(j, d, carry):
        rows = pl.ds(pl.multiple_of(j * TILE, TILE), TILE)
        if d == 0:
            c = conv(j)
            conv_ref[rows, :] = c
        else:
            c = conv_ref[rows, :]
        gates = jnp.dot(c.astype(BF16), wg_ref[d], preferred_element_type=F32) + bg_ref[d]
        r = jax.nn.sigmoid(gates[:, :LRU_WIDTH])
        i = jax.nn.sigmoid(gates[:, LRU_WIDTH:])
        log_a = (-LRU_C * r) * jax.nn.softplus(-lam_ref[d])
        a = jnp.exp(log_a)
        y = -_expm1(2.0 * log_a, a * a)
        root = jnp.where(y > 0.0, y * lax.rsqrt(y), 0.0)
        bt = root * (i * c)
        return _chunk_scan(a, bt, carry, sub, reverse=(d == 1))

    def fwd(j, carry):
        h, carry = direction(j, 0, carry)
        hs_ref[pl.ds(pl.multiple_of(j * TILE, TILE), TILE), :] = h
        return carry

    def rev(n, carry):
        j = jnp.where(n == 0, 0, nt - n)
        h, carry = direction(j, 1, carry)
        r0 = pl.multiple_of(j * TILE, TILE)
        gate = x_ref[pl.ds(r0, TILE), LRU_WIDTH:]
        o_ref[pl.ds(r0, TILE), :] = ((hs_ref[pl.ds(r0, TILE), :] + h) * gate).astype(BF16)
        return carry

    zero = jnp.zeros((1, LRU_WIDTH), F32)
    lax.fori_loop(0, nt, fwd, zero)
    lax.fori_loop(0, nt, rev, zero)


def _lru(lru_in, conv_w, conv_b, w_gate, b_gate, lam):
    B, T, _ = lru_in.shape
    return pl.pallas_call(
        _lru_kernel,
        grid=(B,),
        in_specs=[
            pl.BlockSpec((None, T, 2 * LRU_WIDTH), lambda b: (b, 0, 0)),
            _const_spec((CONV_WIDTH, LRU_WIDTH)),
            _const_spec((1, LRU_WIDTH)),
            _const_spec((2, LRU_WIDTH, 2 * LRU_WIDTH)),
            _const_spec((2, 1, 2 * LRU_WIDTH)),
            _const_spec((2, 1, LRU_WIDTH)),
        ],
        out_specs=pl.BlockSpec((None, T, LRU_WIDTH), lambda b: (b, 0, 0)),
        out_shape=jax.ShapeDtypeStruct((B, T, LRU_WIDTH), BF16),
        scratch_shapes=[pltpu.VMEM((T, LRU_WIDTH), F32),
                        pltpu.VMEM((T, LRU_WIDTH), F32)],
        compiler_params=_cparams(("parallel",)),
        name="rglru",
    )(lru_in, conv_w, conv_b, w_gate, b_gate, lam)


def _ffn_kernel(*refs, t_off, n_tiles, final):
    _run_subtiles(lambda n_sub: _ffn_tiles(refs, n_sub, t_off, final), t_off, n_tiles)


def _ffn_tiles(refs, n_sub, t_off, final):
    per_sub = 5
    subs = [refs[per_sub * j:per_sub * (j + 1)] for j in range(n_sub)]
    xc_ref, g2_ref, gf_ref, wo_ref, wg_ref, wu_ref, wd_ref, o_ref = refs[per_sub * SUBTILES:]
    first_tile = SUBTILES * pl.program_id(1) + t_off

    proj = [jnp.dot(jnp.concatenate([ya[...], yr[...], ys[...]], axis=-1), wo_ref[...],
                    preferred_element_type=F32) for ya, yr, ys, _, _ in subs]
    x1, g, u = [], [], []
    for j, (_, _, _, xl_ref, mod_ref) in enumerate(subs):
        x = jnp.where(first_tile + j == 0, xc_ref[...], xl_ref[...])
        x1.append(x + mod_ref[2] * proj[j])
        rs = lax.rsqrt(jnp.mean(x1[j] * x1[j], axis=-1, keepdims=True) + EPS)
        h = ((x1[j] * rs * g2_ref[...]) * (1.0 + mod_ref[4]) + mod_ref[3]).astype(BF16)
        g.append(jnp.dot(h, wg_ref[...], preferred_element_type=F32))
        u.append(jnp.dot(h, wu_ref[...], preferred_element_type=F32))
    for j, (_, _, _, _, mod_ref) in enumerate(subs):
        act = ((g[j] * jax.nn.sigmoid(g[j])) * u[j]).astype(BF16)
        x2 = x1[j] + mod_ref[5] * jnp.dot(act, wd_ref[...], preferred_element_type=F32)
        if final:
            x2 = x2 * lax.rsqrt(jnp.mean(x2 * x2, axis=-1, keepdims=True) + EPS) * gf_ref[...]
        o_ref[j * TILE:(j + 1) * TILE, :] = x2


def _ffn(ya, yr, ys, x_ctx, x_lat, lat_first, mod, g2, g_final, layer, w_out, w_gate, w_up,
         w_down, t_off, final):
    B, T, _ = yr.shape
    n_tiles = T // TILE
    nt = n_tiles - t_off
    with_ctx = t_off == 0
    ya_block = ((lambda t: jnp.where(t == 0, 0, ATTN_Q_TILES + t - 1)) if with_ctx
                else (lambda t: t - 1))
    kern = functools.partial(_ffn_kernel, t_off=t_off, n_tiles=n_tiles, final=final)

    def sub_specs(j):
        tile = lambda i: _sub_tile(i, j, t_off, n_tiles)
        return [pl.BlockSpec((None, TILE, ATTN_WIDTH), lambda b, i: (b, ya_block(tile(i)), 0)),
                _token_spec(tile, LRU_WIDTH),
                _token_spec(tile, SGU_WIDTH), _latent_spec(tile, lat_first), _mod_spec(tile)]

    tokens = []
    for j in range(SUBTILES):
        tokens += [ya, yr, ys, x_lat, mod]
    return pl.pallas_call(
        kern,
        grid=(B, pl.cdiv(nt, SUBTILES)),
        in_specs=[
            *[spec for j in range(SUBTILES) for spec in sub_specs(j)],
            _CTX_SPEC,
            _const_spec((1, D_MODEL)),
            _const_spec((1, D_MODEL)),
            _layer_spec(layer, (D_MODEL, D_MODEL)),
            _layer_spec(layer, (D_MODEL, FFN_HIDDEN)),
            _layer_spec(layer, (D_MODEL, FFN_HIDDEN)),
            _layer_spec(layer, (FFN_HIDDEN, D_MODEL)),
        ],
        out_specs=pl.BlockSpec((None, SUBTILES * TILE, D_MODEL), lambda b, i: (b, i, 0)),
        out_shape=jax.ShapeDtypeStruct((B, nt * TILE, D_MODEL), F32),
        compiler_params=_cparams(("parallel", "parallel")),
        name="outproj_ffn",
    )(*tokens, x_ctx, g2, g_final, w_out, w_gate, w_up, w_down)


def _rope_tables(seq):
    rows = seq // GRID_W
    row = jnp.repeat(jnp.arange(rows), GRID_W).astype(F32)
    col = jnp.tile(jnp.arange(GRID_W), rows).astype(F32)
    n = ROPE_AXIS_DIM // 2
    inv = ROPE_THETA ** (-jnp.arange(n, dtype=F32) / n)
    ang_r = row[:, None] * inv[None, :]
    ang_c = col[:, None] * inv[None, :]
    cr, sr, cc, sc = jnp.cos(ang_r), jnp.sin(ang_r), jnp.cos(ang_c), jnp.sin(ang_c)
    cos = jnp.tile(jnp.concatenate([cr, cr, cc, cc], axis=1), (1, 2))
    sin = jnp.tile(jnp.concatenate([-sr, sr, -sc, sc], axis=1), (1, 2))
    cos = jnp.concatenate([jnp.ones((CTX_LEN, HEAD_DIM), F32), cos], axis=0)
    sin = jnp.concatenate([jnp.zeros((CTX_LEN, HEAD_DIM), F32), sin], axis=0)
    return cos, sin


def _block_diag(w):
    eye = jnp.eye(w.shape[0], dtype=w.dtype)
    return jnp.einsum('hij,hk->hikj', w, eye).reshape(w.shape[0] * w.shape[1],
                                                      w.shape[0] * w.shape[2])


def kernel(x, c, ctx, c_ctx, w_ada, b_ada, g_norm1, g_norm2, w_in, lam_q, lam_k, g_attn, conv_w,
           conv_b, w_rg_a, b_rg_a, w_rg_x, b_rg_x, lru_lambda, g_sgu, w_spatial, b_spatial, w_out,
           w_ffn_gate, w_ffn_up, w_ffn_down, g_final):
    B, S, D = x.shape
    assert (D, ctx.shape[1], S % KEY_CHUNK) == (D_MODEL, CTX_LEN, 0) and B <= 8
    cos_t, sin_t = _rope_tables(S)
    cs = jnp.zeros((16, D), F32).at[:B].set(c).at[8].set(c_ctx)
    mods = _ada(cs, w_ada, b_ada)
    ones_bd = _block_diag(jnp.ones((SGU_GROUPS, SGU_GROUP_DIM, SGU_GROUP_DIM), BF16))
    w_in16, w_out16 = w_in.astype(BF16), w_out.astype(BF16)
    w_gate16, w_up16, w_down16 = (w.astype(BF16) for w in (w_ffn_gate, w_ffn_up, w_ffn_down))
    stream = (ctx, x, 1)

    for l in range(DEPTH):
        last = l == DEPTH - 1
        lam_init = 0.8 - 0.6 * math.exp(-0.3 * l)
        mod = mods[l].reshape(16, N_MOD, 1, D)
        w_sp = jnp.transpose(w_spatial[l], (1, 0, 2)).reshape(CHUNK, SGU_GROUPS * CHUNK)
        b_sp = jnp.repeat(b_spatial[l].T, SGU_GROUP_DIM, axis=1)
        w_gate = jnp.stack([jnp.concatenate([_block_diag(w_rg_a[l, d]), _block_diag(w_rg_x[l, d])],
                                            axis=1) for d in range(2)]).astype(BF16)
        b_gate = jnp.concatenate([b_rg_a[l], b_rg_x[l]], axis=1)[:, None, :]

        qt, k, vt, lru_in, ys = _inproj(
            *stream, mod, g_norm1[l][None], l, w_in16, cos_t, sin_t, g_sgu[l][None],
            ones_bd, w_sp.astype(BF16), b_sp)
        t_off = 1 if last else 0
        ya = _attention(qt, k, vt, lam_q[l], lam_k[l], g_attn[l][None], lam_init, not last)
        yr = _lru(lru_in, conv_w[l], conv_b[l][None], w_gate, b_gate, lru_lambda[l][:, None, :])
        xs = _ffn(ya, yr, ys, *stream, mod, g_norm2[l][None], g_final[None], l, w_out16,
                  w_gate16, w_up16, w_down16, t_off, last)
        stream = (xs, xs, 0)
    return xs
```
